```python
import jax, jax.numpy as jnp
from jax import lax
import numpy as np

D_MODEL = 2048
BATCH = 16
SEQ = 2048
DEPTH = 4

HEAD_DIM = 128
NORM_EPS = 1e-6
NEG_INF = -1e30
Q_BLOCK = 128
NSA_HEADS = 6
NSA_KV_GROUPS = 2
NSA_HPG = NSA_HEADS // NSA_KV_GROUPS
NSA_BRANCHES = 3
CMP_BLOCK = 32
CMP_STRIDE = 16
SEL_BLOCK = 64
SEL_TOP = 8
SEL_Q_CHUNK = 64
SEL_FORCE = 1e4
WIN = 512
DIL_PATTERNS = ((128, 1), (512, 4), (2048, 16))
DIL_HPG = 2
DIL_HEADS = DIL_HPG * len(DIL_PATTERNS)
POOL_WINDOWS = (2, 4, 8, 16)
POOL_GROUP = 128
POOL_WIDTH = POOL_GROUP * len(POOL_WINDOWS)
NSA_Q_W = NSA_HEADS * HEAD_DIM
NSA_KV_W = NSA_BRANCHES * 2 * NSA_KV_GROUPS * HEAD_DIM
NSA_GATE_W = NSA_BRANCHES * NSA_HEADS
DIL_QKV_W = 3 * DIL_HEADS * HEAD_DIM
MERGE_GATE_W = 3 * D_MODEL
D_IN = NSA_Q_W + NSA_KV_W + NSA_GATE_W + DIL_QKV_W + POOL_WIDTH + MERGE_GATE_W
NSA_OUT_W = NSA_HEADS * HEAD_DIM
DIL_OUT_W = DIL_HPG * HEAD_DIM
D_FF = 5632
N_EXPERTS = 8
TOP_K = 2

kernel_name = 'hybrid_nsa_dilated_pool_moe_trunk'


def _rms_norm(x, g):
    xf = x.astype(jnp.float32)
    y = xf * lax.rsqrt(jnp.mean(xf * xf, axis=-1, keepdims=True) + NORM_EPS)
    return (y * g.astype(jnp.float32)).astype(x.dtype)


def _alibi_slopes(n):
    return jnp.asarray(2.0 ** (-8.0 * np.arange(1, n + 1) / n), dtype=jnp.float32)


def _cmp_to_sel_overlap(n_cmp, n_sel):
    i = np.arange(n_cmp)[:, None] * CMP_STRIDE
    j = np.arange(n_sel)[None, :] * SEL_BLOCK
    ov = np.minimum(i + CMP_BLOCK, j + SEL_BLOCK) - np.maximum(i, j)
    return jnp.asarray(np.clip(ov, 0, None) / CMP_BLOCK, dtype=jnp.float32)


def _compress(x, w1, b1, w2):
    B, S, G, D = x.shape
    c = x.reshape(B, S // CMP_STRIDE, CMP_STRIDE, G, D)
    lo = jnp.einsum('bnlgd,lde->bnge', c, w1[:CMP_STRIDE])
    hi = jnp.einsum('bnlgd,lde->bnge', c, w1[CMP_STRIDE:])
    h = jax.nn.gelu(lo[:, :-1] + hi[:, 1:] + b1)
    return jnp.einsum('bnge,ef->bngf', h, w2)


def _nsa_mixer(q, kv, gate_logits, q_gain, k_gain, w1, b1, w2):
    B, S = q.shape[:2]
    G, HPG, D = NSA_KV_GROUPS, NSA_HPG, HEAD_DIM
    f32 = jnp.float32
    scale = D ** -0.5
    slopes = _alibi_slopes(NSA_HEADS).reshape(G, HPG)
    qg = _rms_norm(q, q_gain).reshape(B, S, G, HPG, D)
    t = jnp.arange(S)

    k_c = _rms_norm(_compress(kv[:, :, 0, 0], w1[0], b1[0], w2[0]), k_gain[0])
    v_c = _compress(kv[:, :, 0, 1], w1[1], b1[1], w2[1])
    n_cmp = k_c.shape[1]
    dist_c = t[:, None] - (jnp.arange(n_cmp) * CMP_STRIDE + CMP_BLOCK - 1)[None, :]
    valid_c = dist_c >= 0
    s_c = jnp.einsum('bsghd,bngd->bghsn', qg, k_c).astype(f32) * scale - slopes[:, :, None, None] * dist_c.astype(f32)
    p_cmp = jax.nn.softmax(jnp.where(valid_c, s_c, NEG_INF), axis=-1) * valid_c
    o_cmp = jnp.einsum('bghsn,bngd->bsghd', p_cmp.astype(v_c.dtype), v_c)

    n_sel = S // SEL_BLOCK
    n_top = min(SEL_TOP, n_sel)
    imp = jnp.einsum('bghsn,nj->bgsj', p_cmp, _cmp_to_sel_overlap(n_cmp, n_sel))
    blk = jnp.arange(n_sel)[None, :]
    cur = (t // SEL_BLOCK)[:, None]
    forced = (blk == 0) | (blk == cur) | (blk == cur - 1)
    imp = jnp.where(forced, SEL_FORCE, jnp.where(blk <= cur, imp, -1.0))
    sel_idx = lax.top_k(imp, n_top)[1]
    k_s = _rms_norm(kv[:, :, 1, 0], k_gain[1]).reshape(B, n_sel, SEL_BLOCK, G, D).transpose(0, 3, 1, 2, 4)
    v_s = kv[:, :, 1, 1].reshape(B, n_sel, SEL_BLOCK, G, D).transpose(0, 3, 1, 2, 4)
    n_qc = S // SEL_Q_CHUNK
    q_ch = qg.reshape(B, n_qc, SEL_Q_CHUNK, G, HPG, D).transpose(1, 0, 2, 3, 4, 5)
    i_ch = sel_idx.reshape(B, G, n_qc, SEL_Q_CHUNK, n_top).transpose(2, 0, 1, 3, 4)
    t_ch = t.reshape(n_qc, SEL_Q_CHUNK)
    bi = jnp.arange(B)[:, None, None, None]
    gi = jnp.arange(G)[None, :, None, None]

    def sel_chunk(args):
        qc, ic, tc = args
        kg = k_s[bi, gi, ic]
        vg = v_s[bi, gi, ic]
        pos = ic[..., None] * SEL_BLOCK + jnp.arange(SEL_BLOCK)
        dist = tc[None, None, :, None, None] - pos
        s = jnp.einsum('bcghd,bgcnld->bghcnl', qc, kg).astype(f32) * scale - slopes[None, :, :, None, None, None] * dist[:, :, None].astype(f32)
        s = jnp.where((dist >= 0)[:, :, None], s, NEG_INF)
        p = jax.nn.softmax(s.reshape(s.shape[:4] + (-1,)), axis=-1).reshape(s.shape)
        return jnp.einsum('bghcnl,bgcnld->bcghd', p.astype(vg.dtype), vg)

    o_slc = lax.map(sel_chunk, (q_ch, i_ch, t_ch)).transpose(1, 0, 2, 3, 4, 5).reshape(B, S, G, HPG, D)

    pad = ((0, 0), (WIN, 0), (0, 0), (0, 0))
    k_w = jnp.pad(_rms_norm(kv[:, :, 2, 0], k_gain[2]), pad)
    v_w = jnp.pad(kv[:, :, 2, 1], pad)
    n_qb = S // Q_BLOCK
    q_bl = qg.reshape(B, n_qb, Q_BLOCK, G, HPG, D).transpose(1, 0, 2, 3, 4, 5)
    starts = jnp.arange(n_qb) * Q_BLOCK

    def win_block(args):
        qb, q0 = args
        kb = lax.dynamic_slice_in_dim(k_w, q0, Q_BLOCK + WIN, axis=1)
        vb = lax.dynamic_slice_in_dim(v_w, q0, Q_BLOCK + WIN, axis=1)
        tq = q0 + jnp.arange(Q_BLOCK)
        sk = q0 - WIN + jnp.arange(Q_BLOCK + WIN)
        dist = tq[:, None] - sk[None, :]
        valid = (dist >= 0) & (dist < WIN) & (sk >= 0)[None, :]
        s = jnp.einsum('bqghd,bkgd->bghqk', qb, kb).astype(f32) * scale - slopes[:, :, None, None] * dist.astype(f32)
        p = jax.nn.softmax(jnp.where(valid, s, NEG_INF), axis=-1)
        return jnp.einsum('bghqk,bkgd->bqghd', p.astype(vb.dtype), vb)

    o_win = lax.map(win_block, (q_bl, starts)).transpose(1, 0, 2, 3, 4, 5).reshape(B, S, G, HPG, D)

    gates = jax.nn.sigmoid(gate_logits.astype(f32)).astype(q.dtype).reshape(B, S, NSA_BRANCHES, G, HPG, 1)
    o = gates[:, :, 0] * o_cmp + gates[:, :, 1] * o_slc + gates[:, :, 2] * o_win
    return o.reshape(B, S, NSA_OUT_W)


def _dilated_mixer(qkv, q_gain, k_gain):
    B, S = qkv.shape[:2]
    D = HEAD_DIM
    f32 = jnp.float32
    scale = D ** -0.5
    slopes = _alibi_slopes(DIL_HEADS).reshape(len(DIL_PATTERNS), DIL_HPG)
    q = _rms_norm(qkv[:, :, 0], q_gain)
    k = _rms_norm(qkv[:, :, 1], k_gain)
    v = qkv[:, :, 2]
    n_qb = S // Q_BLOCK
    starts = jnp.arange(n_qb) * Q_BLOCK
    outs, lses = [], []
    for g, (w, r) in enumerate(DIL_PATTERNS):
        hs = slice(g * DIL_HPG, (g + 1) * DIL_HPG)
        pad = ((0, 0), (w, 0), (0, 0), (0, 0))
        k_p = jnp.pad(k[:, :, hs], pad)
        v_p = jnp.pad(v[:, :, hs], pad)
        q_bl = q[:, :, hs].reshape(B, n_qb, Q_BLOCK, DIL_HPG, D).transpose(1, 0, 2, 3, 4)
        offs = jnp.arange(w // r + 1) * r
        bias = -slopes[g][:, None, None] * offs.astype(f32)

        def dil_block(args, k_p=k_p, v_p=v_p, w=w, offs=offs, bias=bias):
            qb, q0 = args
            pos = q0 + jnp.arange(Q_BLOCK)[:, None] - offs[None, :]
            kb = jnp.take(k_p, pos + w, axis=1)
            vb = jnp.take(v_p, pos + w, axis=1)
            s = jnp.einsum('bqhd,bqjhd->bhqj', qb, kb).astype(f32) * scale + bias
            s = jnp.where(pos >= 0, s, NEG_INF)
            m = jnp.max(s, axis=-1, keepdims=True)
            e = jnp.exp(s - m)
            den = jnp.sum(e, axis=-1, keepdims=True)
            o = jnp.einsum('bhqj,bqjhd->bqhd', (e / den).astype(vb.dtype), vb)
            lse = (m + jnp.log(den))[..., 0].transpose(0, 2, 1)
            return o, lse

        o, lse = lax.map(dil_block, (q_bl, starts))
        outs.append(o.transpose(1, 0, 2, 3, 4).reshape(B, S, DIL_HPG, D))
        lses.append(lse.transpose(1, 0, 2, 3).reshape(B, S, DIL_HPG))
    alpha = jax.nn.softmax(jnp.stack(lses), axis=0)
    y = jnp.einsum('pbsh,pbshd->bshd', alpha.astype(v.dtype), jnp.stack(outs))
    return y.reshape(B, S, DIL_OUT_W)


def _pool_mixer(u, w_pool, scale):
    B, S, _ = u.shape
    uf = u.astype(jnp.float32)
    cs = jnp.concatenate([jnp.zeros((B, 1, POOL_WIDTH), jnp.float32), jnp.cumsum(uf, axis=1)], axis=1)
    t = jnp.arange(S)
    parts = []
    for g, w in enumerate(POOL_WINDOWS):
        cg = slice(g * POOL_GROUP, (g + 1) * POOL_GROUP)
        cs_g = cs[:, :, cg]
        lo = jnp.maximum(t + 1 - w, 0)
        mean = (cs_g[:, 1:] - jnp.take(cs_g, lo, axis=1)) / jnp.minimum(t + 1, w).astype(jnp.float32)[None, :, None]
        parts.append(mean - uf[:, :, cg])
    d = jnp.stack(parts, axis=2).astype(u.dtype)
    y = jnp.einsum('bsgc,gce->bsge', d, w_pool).reshape(B, S, POOL_WIDTH)
    return y * scale


def _swiglu(h, w_gate, w_up, w_down):
    return (jax.nn.silu(h @ w_gate) * (h @ w_up)) @ w_down


def _moe(h, w_router, b_router, w_gate, w_up, w_down):
    logits = (h @ w_router).astype(jnp.float32) + b_router.astype(jnp.float32)
    top_val, top_idx = lax.top_k(logits, TOP_K)
    comb = jnp.einsum('bsk,bske->bse', jax.nn.softmax(top_val, axis=-1), jax.nn.one_hot(top_idx, N_EXPERTS, dtype=jnp.float32)).astype(h.dtype)
    y = jnp.zeros_like(h)
    for e in range(N_EXPERTS):
        y = y + comb[..., e:e + 1] * _swiglu(h, w_gate[e], w_up[e], w_down[e])
    return y


def setup_inputs(seed: int = 0) -> dict:
    key = jax.random.key(seed)
    ks = jax.random.split(key, 26)
    f32 = jnp.float32
    D, HD = D_MODEL, HEAD_DIM
    n_dense = (DEPTH + 1) // 2
    n_moe = DEPTH // 2

    def nrm(k, shape, fan_in):
        return jax.random.normal(k, shape, f32) * (fan_in ** -0.5)

    def gain(k, shape):
        return 1.0 + 0.05 * jax.random.normal(k, shape, f32)

    return {
        'x': jax.random.normal(ks[0], (BATCH, SEQ, D), f32),
        'mix_norm': gain(ks[1], (DEPTH, D)),
        'w_in': nrm(ks[2], (DEPTH, D, D_IN), D),
        'nsa_q_norm': gain(ks[3], (DEPTH, HD)),
        'nsa_k_norm': gain(ks[4], (DEPTH, NSA_BRANCHES, HD)),
        'cmp_w1': nrm(ks[5], (DEPTH, 2, CMP_BLOCK, HD, HD), CMP_BLOCK * HD),
        'cmp_b1': 0.02 * jax.random.normal(ks[6], (DEPTH, 2, HD), f32),
        'cmp_w2': nrm(ks[7], (DEPTH, 2, HD, HD), HD),
        'dil_q_norm': gain(ks[8], (DEPTH, HD)),
        'dil_k_norm': gain(ks[9], (DEPTH, HD)),
        'pool_w': nrm(ks[10], (DEPTH, len(POOL_WINDOWS), POOL_GROUP, POOL_GROUP), POOL_GROUP),
        'pool_scale': gain(ks[11], (DEPTH, POOL_WIDTH)),
        'w_branch_a': nrm(ks[12], (DEPTH, NSA_OUT_W, D), NSA_OUT_W),
        'w_branch_b': nrm(ks[13], (DEPTH, DIL_OUT_W, D), DIL_OUT_W),
        'w_branch_c': nrm(ks[14], (DEPTH, POOL_WIDTH, D), POOL_WIDTH),
        'w_out': nrm(ks[15], (DEPTH, D, D), D),
        'ffn_norm': gain(ks[16], (DEPTH, D)),
        'ffn_w_gate': nrm(ks[17], (n_dense, D, D_FF), D),
        'ffn_w_up': nrm(ks[18], (n_dense, D, D_FF), D),
        'ffn_w_down': nrm(ks[19], (n_dense, D_FF, D), D_FF),
        'moe_w_router': nrm(ks[20], (n_moe, D, N_EXPERTS), D),
        'moe_b_router': 0.01 * jax.random.normal(ks[21], (n_moe, N_EXPERTS), f32),
        'moe_w_gate': nrm(ks[22], (n_moe, N_EXPERTS, D, D_FF), D),
        'moe_w_up': nrm(ks[23], (n_moe, N_EXPERTS, D, D_FF), D),
        'moe_w_down': nrm(ks[24], (n_moe, N_EXPERTS, D_FF, D), D_FF),
    }


def reference(x, mix_norm, w_in, nsa_q_norm, nsa_k_norm, cmp_w1, cmp_b1, cmp_w2, dil_q_norm, dil_k_norm, pool_w, pool_scale, w_branch_a, w_branch_b, w_branch_c, w_out, ffn_norm, ffn_w_gate, ffn_w_up, ffn_w_down, moe_w_router, moe_b_router, moe_w_gate, moe_w_up, moe_w_down):
    B, S, _ = x.shape
    splits = np.cumsum([NSA_Q_W, NSA_KV_W, NSA_GATE_W, DIL_QKV_W, POOL_WIDTH]).tolist()
    for l in range(DEPTH):
        h = _rms_norm(x, mix_norm[l])
        z = h @ w_in[l]
        z_q, z_kv, z_g, z_dil, z_pool, z_merge = jnp.split(z, splits, axis=-1)
        y_a = _nsa_mixer(z_q.reshape(B, S, NSA_HEADS, HEAD_DIM),
                         z_kv.reshape(B, S, NSA_BRANCHES, 2, NSA_KV_GROUPS, HEAD_DIM),
                         z_g, nsa_q_norm[l], nsa_k_norm[l], cmp_w1[l], cmp_b1[l], cmp_w2[l])
        y_b = _dilated_mixer(z_dil.reshape(B, S, 3, DIL_HEADS, HEAD_DIM), dil_q_norm[l], dil_k_norm[l])
        y_c = _pool_mixer(z_pool, pool_w[l], pool_scale[l])
        gate = jax.nn.sigmoid(z_merge.astype(jnp.float32)).astype(x.dtype).reshape(B, S, 3, D_MODEL)
        merged = (gate[:, :, 0] * (y_a @ w_branch_a[l])
                  + gate[:, :, 1] * (y_b @ w_branch_b[l])
                  + gate[:, :, 2] * (y_c @ w_branch_c[l]))
        x = x + merged @ w_out[l]
        h = _rms_norm(x, ffn_norm[l])
        i = l // 2
        if l % 2 == 0:
            x = x + _swiglu(h, ffn_w_gate[i], ffn_w_up[i], ffn_w_down[i])
        else:
            x = x + _moe(h, moe_w_router[i], moe_b_router[i], moe_w_gate[i], moe_w_up[i], moe_w_down[i])
    return x
```

```python
import functools

import numpy as np
import jax
import jax.numpy as jnp
from jax import lax
from jax.experimental import pallas as pl
from jax.experimental.pallas import tpu as pltpu

F32 = jnp.float32
BF16 = jnp.bfloat16

HEAD_DIM = 128
NORM_EPS = 1e-6
NEG_INF = -1e30
NSA_HEADS = 6
NSA_KV_GROUPS = 2
NSA_HPG = NSA_HEADS // NSA_KV_GROUPS
NSA_BRANCHES = 3
CMP_BLOCK = 32
CMP_STRIDE = 16
SEL_BLOCK = 64
SEL_TOP = 8
SEL_FORCE = 1e4
WIN = 512
DIL_PATTERNS = ((128, 1), (512, 4), (2048, 16))
DIL_HPG = 2
DIL_HEADS = DIL_HPG * len(DIL_PATTERNS)
DIL_SPAN = 128
POOL_WINDOWS = (2, 4, 8, 16)
N_EXPERTS = 8
TOP_K = 2
SCALE = HEAD_DIM ** -0.5

LANES = 128
MERGE_SLABS = 3 * 16
Q_SLAB0 = MERGE_SLABS
KV_SLAB0 = Q_SLAB0 + NSA_HEADS
DIL_SLAB0 = KV_SLAB0 + NSA_BRANCHES * 2 * NSA_KV_GROUPS
POOL_SLAB0 = DIL_SLAB0 + 3 * DIL_HEADS
N_SLABS = POOL_SLAB0 + len(POOL_WINDOWS)

VMEM_LIMIT = 56 * 1024 * 1024


def _cparams(sem):
    return pltpu.CompilerParams(dimension_semantics=sem, vmem_limit_bytes=VMEM_LIMIT)


def _rms(x, g):
    return x * lax.rsqrt(jnp.mean(x * x, axis=-1, keepdims=True) + NORM_EPS) * g


def _dot(a, b):
    return jnp.dot(a, b, preferred_element_type=F32)


def _dot_nt(a, b):
    return lax.dot_general(a, b, (((1,), (1,)), ((), ())), preferred_element_type=F32)


def _alibi_slopes(n):
    return jnp.asarray(2.0 ** (-8.0 * np.arange(1, n + 1) / n), dtype=F32)


IN_TN = 1024
IN_SLABS_PER_TILE = IN_TN // LANES
IN_SIG_TILES = MERGE_SLABS // IN_SLABS_PER_TILE


def _in_proj_kernel(x_ref, g_ref, w_ref, wg_ref, z_ref, gate_ref, h_ref):
    j = pl.program_id(1)

    @pl.when(j == 0)
    def _():
        h = _rms(x_ref[...], g_ref[...]).astype(BF16)
        h_ref[...] = h
        for g in range(NSA_KV_GROUPS):
            gate_ref[g] = jax.nn.sigmoid(_dot(h, wg_ref[g]))

    acc = _dot(h_ref[...], w_ref[...])

    def write(a):
        for c in range(IN_SLABS_PER_TILE):
            z_ref[c] = a[:, c * LANES:(c + 1) * LANES].astype(BF16)

    @pl.when(j < IN_SIG_TILES)
    def _():
        write(jax.nn.sigmoid(acc))

    @pl.when(j >= IN_SIG_TILES)
    def _():
        write(acc)


def _in_proj(x2, g, w_main, w_gate, tm=512):
    n, d = x2.shape
    n_tiles = w_main.shape[1] // IN_TN
    return pl.pallas_call(
        _in_proj_kernel,
        grid=(n // tm, n_tiles),
        in_specs=[
            pl.BlockSpec((tm, d), lambda i, j: (i, 0)),
            pl.BlockSpec((1, d), lambda i, j: (0, 0)),
            pl.BlockSpec((d, IN_TN), lambda i, j: (0, j)),
            pl.BlockSpec((NSA_KV_GROUPS, d, LANES), lambda i, j: (0, 0, 0)),
        ],
        out_specs=[
            pl.BlockSpec((IN_SLABS_PER_TILE, tm, LANES), lambda i, j: (j, i, 0)),
            pl.BlockSpec((NSA_KV_GROUPS, tm, LANES), lambda i, j: (0, i, 0)),
        ],
        out_shape=[
            jax.ShapeDtypeStruct((N_SLABS, n, LANES), BF16),
            jax.ShapeDtypeStruct((NSA_KV_GROUPS, n, LANES), F32),
        ],
        scratch_shapes=[pltpu.VMEM((tm, d), BF16)],
        compiler_params=_cparams(("parallel", "arbitrary")),
    )(x2, g, w_main, w_gate)


NSA_TQ = 256


def _softmax_parts(s):
    m = jnp.max(s, axis=-1, keepdims=True)
    e = jnp.exp(s - m)
    return e, jnp.sum(e, axis=-1, keepdims=True)


def _nsa_kernel(slope_ref, q_ref, kc_ref, vc_ref, ks_ref, vs_ref, kw_ref, vw_ref, gate_ref,
                qg_ref, kg_ref, w1_ref, b1_ref, w2_ref, ovt_ref, esel_ref, o_ref,
                tmp32, kcn_s, vcn_s, ksn_s, kwn_s):
    g = pl.program_id(1)
    qi = pl.program_id(2)
    tq = NSA_TQ
    seq = ks_ref.shape[2]
    n_cmp_pad = seq // CMP_STRIDE
    n_sel = seq // SEL_BLOCK

    @pl.when(qi == 0)
    def _():
        for which, (src, dst) in enumerate(((kc_ref, kcn_s), (vc_ref, vcn_s))):
            tmp32[...] = src[0, 0].astype(F32)
            lo = jnp.zeros((n_cmp_pad, HEAD_DIM), F32)
            hi = jnp.zeros((n_cmp_pad, HEAD_DIM), F32)
            for l in range(CMP_STRIDE):
                xl = tmp32[pl.ds(l, n_cmp_pad, stride=CMP_STRIDE), :].astype(BF16)
                lo = lo + _dot(xl, w1_ref[which, l])
                hi = hi + _dot(xl, w1_ref[which, CMP_STRIDE + l])
            hid = jax.nn.gelu(lo + pltpu.roll(hi, n_cmp_pad - 1, axis=0) + b1_ref[which:which + 1, :])
            c = _dot(hid.astype(BF16), w2_ref[which])
            if which == 0:
                c = _rms(c, kg_ref[0:1, :])
            dst[...] = c.astype(BF16)
        ksn_s[...] = _rms(ks_ref[0, 0].astype(F32), kg_ref[1:2, :]).astype(BF16)
        kwn_s[...] = _rms(kw_ref[0, 0].astype(F32), kg_ref[2:3, :]).astype(BF16)

    q0 = qi * tq
    t_col = q0 + lax.broadcasted_iota(jnp.int32, (tq, 1), 0)
    slopes = [slope_ref[g * NSA_HPG + hh] for hh in range(NSA_HPG)]
    qn = [_rms(q_ref[hh, 0].astype(F32), qg_ref[...]).astype(BF16) for hh in range(NSA_HPG)]


    n_row = lax.broadcasted_iota(jnp.int32, (1, n_cmp_pad), 1)
    c_end = n_row * CMP_STRIDE + (CMP_BLOCK - 1)
    valid_c = (c_end <= t_col) & (n_row < n_cmp_pad - 1)
    valid_cf = valid_c.astype(F32)
    c_mask = jnp.where(valid_c, 0.0, NEG_INF)
    c_endf = c_end.astype(F32)
    kcn = kcn_s[...]
    vcn = vcn_s[...]
    psum = jnp.zeros((tq, n_cmp_pad), F32)
    o_cmp = []
    for hh in range(NSA_HPG):
        s = _dot_nt(qn[hh], kcn) * SCALE + (slopes[hh] * c_endf + c_mask)
        e, den = _softmax_parts(s)
        p = e * (valid_cf / den)
        psum = psum + p
        o_cmp.append(_dot(p.astype(BF16), vcn))

    imp_t = lax.dot_general(ovt_ref[...], psum, (((1,), (1,)), ((), ())),
                            precision=lax.Precision.HIGHEST, preferred_element_type=F32)
    blk = lax.broadcasted_iota(jnp.int32, (n_sel, tq), 0)
    cur = lax.shift_right_logical(q0 + lax.broadcasted_iota(jnp.int32, (n_sel, tq), 1),
                                  int(np.log2(SEL_BLOCK)))
    forced = (blk == 0) | (blk == cur) | (blk == cur - 1)
    imp = jnp.where(forced, SEL_FORCE, jnp.where(blk <= cur, imp_t, -1.0))
    rank = jnp.zeros((n_sel, tq), jnp.int32)
    for i in range(n_sel):
        ri = imp[i:i + 1, :]
        beats = (ri > imp) | ((ri == imp) & (blk > i))
        rank = rank + beats.astype(jnp.int32)
    sel = jnp.where(rank < min(SEL_TOP, n_sel), 1.0, 0.0).astype(BF16)
    sel_keys = lax.dot_general(sel, esel_ref[...], (((0,), (0,)), ((), ())),
                               preferred_element_type=F32)
    k_pos = lax.broadcasted_iota(jnp.int32, (1, seq), 1)
    s_mask = jnp.where(k_pos <= t_col, (sel_keys - 1.0) * (-NEG_INF), NEG_INF)
    k_posf = k_pos.astype(F32)

    w_len = WIN + tq
    w0 = pl.multiple_of(jnp.maximum(q0 - WIN, 0), tq)
    kwin = kwn_s[pl.ds(w0, w_len), :]
    vwin = vw_ref[0, 0, pl.ds(w0, w_len), :]
    w_pos = w0 + lax.broadcasted_iota(jnp.int32, (1, w_len), 1)
    w_dist = t_col - w_pos
    w_mask = jnp.where(w_dist >= 0, jnp.where(w_dist < WIN, 0.0, NEG_INF), NEG_INF)
    w_posf = w_pos.astype(F32)

    ksn = ksn_s[...]
    vsel = vs_ref[0, 0]
    gates = gate_ref[0]
    for hh in range(NSA_HPG):
        s = _dot_nt(qn[hh], ksn) * SCALE + (slopes[hh] * k_posf + s_mask)
        e, den = _softmax_parts(s)
        o_sel = _dot(e.astype(BF16), vsel) / den
        s = _dot_nt(qn[hh], kwin) * SCALE + (slopes[hh] * w_posf + w_mask)
        e, den = _softmax_parts(s)
        o_win = _dot(e.astype(BF16), vwin) / den
        out = (gates[:, hh:hh + 1] * o_cmp[hh]
               + gates[:, NSA_HPG + hh:NSA_HPG + hh + 1] * o_sel
               + gates[:, 2 * NSA_HPG + hh:2 * NSA_HPG + hh + 1] * o_win)
        o_ref[:, hh * HEAD_DIM:(hh + 1) * HEAD_DIM] = out.astype(BF16)


def _nsa(z4, gates, q_gain, k_gain, w1, b1, w2, slopes, ovt, esel):
    _, bsz, seq, _ = z4.shape
    tq = NSA_TQ
    nq = seq // tq
    n = bsz * seq

    def kv_spec(br, kvi):
        base = KV_SLAB0 + br * 2 * NSA_KV_GROUPS + kvi * NSA_KV_GROUPS
        return pl.BlockSpec((1, 1, seq, HEAD_DIM), lambda b, g, qi: (base + g, b, 0, 0))

    full = lambda shape: pl.BlockSpec(shape, lambda b, g, qi: (0,) * len(shape))
    return pl.pallas_call(
        _nsa_kernel,
        grid=(bsz, NSA_KV_GROUPS, nq),
        in_specs=[
            pl.BlockSpec(memory_space=pltpu.SMEM),
            pl.BlockSpec((NSA_HPG, 1, tq, HEAD_DIM), lambda b, g, qi: (Q_SLAB0 // NSA_HPG + g, b, qi, 0)),
            kv_spec(0, 0), kv_spec(0, 1), kv_spec(1, 0), kv_spec(1, 1), kv_spec(2, 0), kv_spec(2, 1),
            pl.BlockSpec((1, tq, LANES), lambda b, g, qi: (g, b * nq + qi, 0)),
            full((1, HEAD_DIM)), full((NSA_BRANCHES, HEAD_DIM)),
            full((2, CMP_BLOCK, HEAD_DIM, HEAD_DIM)), full((2, HEAD_DIM)), full((2, HEAD_DIM, HEAD_DIM)),
            full(ovt.shape), full(esel.shape),
        ],
        out_specs=pl.BlockSpec((tq, NSA_HPG * HEAD_DIM), lambda b, g, qi: (b * nq + qi, g)),
        out_shape=jax.ShapeDtypeStruct((n, NSA_HEADS * HEAD_DIM), BF16),
        scratch_shapes=[
            pltpu.VMEM((seq, HEAD_DIM), F32),
            pltpu.VMEM((seq // CMP_STRIDE, HEAD_DIM), BF16),
            pltpu.VMEM((seq // CMP_STRIDE, HEAD_DIM), BF16),
            pltpu.VMEM((seq, HEAD_DIM), BF16),
            pltpu.VMEM((seq, HEAD_DIM), BF16),
        ],
        compiler_params=_cparams(("parallel", "parallel", "arbitrary")),
    )(slopes, z4, z4, z4, z4, z4, z4, z4, gates, q_gain, k_gain, w1, b1, w2, ovt, esel)


DIL_TQ = 256


def _dil_kernel(slope_ref, *refs):
    n_pat = len(DIL_PATTERNS)
    qkv_refs = refs[:3 * n_pat]
    qg_ref, kg_ref, o_ref, qs, ks, vs, num_s, m_s, den_s = refs[3 * n_pat:]
    h = pl.program_id(1)
    seq = qs.shape[0]

    for p, (w, r) in enumerate(DIL_PATTERNS):
        assert w // r == DIL_SPAN
        q_ref, k_ref, v_ref = qkv_refs[3 * p:3 * p + 3]
        qs[...] = _rms(q_ref[0, 0].astype(F32), qg_ref[...])
        ks[...] = _rms(k_ref[0, 0].astype(F32), kg_ref[...])
        vs[...] = v_ref[0, 0].astype(F32)
        slope_r = slope_ref[DIL_HPG * p + h] * float(r)
        length = seq // r
        tq = min(DIL_TQ, length)
        nk = min(length, tq + DIL_SPAN)

        def rows(start, size, r=r):
            return pl.ds(start, size) if r == 1 else pl.ds(start, size, stride=r)

        def tile(idx, carry, p=p, r=r, tq=tq, nk=nk, length=length, slope_r=slope_r, rows=rows):
            c = idx // (length // tq)
            ti = idx % (length // tq)
            u0 = ti * tq
            k0 = jnp.maximum(u0 - DIL_SPAN, 0)
            rq = rows(c + u0 * r, tq)
            rk = rows(c + k0 * r, nk)
            q = qs[rq, :].astype(BF16)
            k = ks[rk, :].astype(BF16)
            v = vs[rk, :].astype(BF16)
            du = (u0 + lax.broadcasted_iota(jnp.int32, (tq, 1), 0)) - (k0 + lax.broadcasted_iota(jnp.int32, (1, nk), 1))
            s = _dot_nt(q, k) * SCALE - slope_r * du.astype(F32)
            s = jnp.where(du >= 0, jnp.where(du <= DIL_SPAN, s, NEG_INF), NEG_INF)
            m = jnp.max(s, axis=-1, keepdims=True)
            e = jnp.exp(s - m)
            den = jnp.sum(e, axis=-1, keepdims=True)
            num_s[p, rq, :] = _dot(e.astype(BF16), v)
            m_s[p, rq, :] = jnp.broadcast_to(m, (tq, HEAD_DIM))
            den_s[p, rq, :] = jnp.broadcast_to(den, (tq, HEAD_DIM))
            return carry

        lax.fori_loop(0, r * (length // tq), tile, 0)

    m_all = [m_s[p] for p in range(n_pat)]
    m_max = functools.reduce(jnp.maximum, m_all)
    numer = jnp.zeros((seq, HEAD_DIM), F32)
    denom = jnp.zeros((seq, HEAD_DIM), F32)
    for p in range(n_pat):
        wt = jnp.exp(m_all[p] - m_max)
        numer = numer + wt * num_s[p]
        denom = denom + wt * den_s[p]
    o_ref[...] = (numer / denom).astype(BF16)


def _dil(z4, q_gain, k_gain, slopes):
    _, bsz, seq, _ = z4.shape
    n = bsz * seq
    n_pat = len(DIL_PATTERNS)

    def spec(which, p):
        base = DIL_SLAB0 + which * DIL_HEADS + DIL_HPG * p
        return pl.BlockSpec((1, 1, seq, HEAD_DIM), lambda b, h: (base + h, b, 0, 0))

    qkv_specs = [spec(which, p) for p in range(n_pat) for which in range(3)]
    gain = pl.BlockSpec((1, HEAD_DIM), lambda b, h: (0, 0))
    return pl.pallas_call(
        _dil_kernel,
        grid=(bsz, DIL_HPG),
        in_specs=[pl.BlockSpec(memory_space=pltpu.SMEM)] + qkv_specs + [gain, gain],
        out_specs=pl.BlockSpec((seq, HEAD_DIM), lambda b, h: (b, h)),
        out_shape=jax.ShapeDtypeStruct((n, DIL_HPG * HEAD_DIM), BF16),
        scratch_shapes=[pltpu.VMEM((seq, HEAD_DIM), F32)] * 3
        + [pltpu.VMEM((n_pat, seq, HEAD_DIM), F32)] * 3,
        compiler_params=_cparams(("parallel", "parallel")),
    )(slopes, *([z4] * (3 * n_pat)), q_gain, k_gain)


def _pool_kernel(u_ref, w_ref, sc_ref, o_ref):
    seq = u_ref.shape[2]
    t = lax.broadcasted_iota(jnp.int32, (seq, 1), 0)
    for g, w in enumerate(POOL_WINDOWS):
        u = u_ref[g, 0].astype(F32)
        acc = u
        span = 1
        while span < w:
            acc = acc + jnp.where(t >= span, pltpu.roll(acc, span, axis=0), 0.0)
            span *= 2
        d = acc / jnp.minimum(t + 1, w).astype(F32) - u
        y = _dot(d.astype(BF16), w_ref[g]) * sc_ref[:, g * LANES:(g + 1) * LANES]
        o_ref[:, g * LANES:(g + 1) * LANES] = y.astype(BF16)


def _pool(z4, w_pool, scale):
    _, bsz, seq, _ = z4.shape
    ng = len(POOL_WINDOWS)
    assert all(w & (w - 1) == 0 for w in POOL_WINDOWS) and POOL_SLAB0 % ng == 0
    return pl.pallas_call(
        _pool_kernel,
        grid=(bsz,),
        in_specs=[
            pl.BlockSpec((ng, 1, seq, LANES), lambda b: (POOL_SLAB0 // ng, b, 0, 0)),
            pl.BlockSpec((ng, LANES, LANES), lambda b: (0, 0, 0)),
            pl.BlockSpec((1, ng * LANES), lambda b: (0, 0)),
        ],
        out_specs=pl.BlockSpec((seq, ng * LANES), lambda b: (b, 0)),
        out_shape=jax.ShapeDtypeStruct((bsz * seq, ng * LANES), BF16),
        compiler_params=_cparams(("parallel",)),
    )(z4, w_pool, scale)


MERGE_TN = 512


def _merge_kernel(ya_ref, yb_ref, yc_ref, wa_ref, wb_ref, wc_ref, g0_ref, g1_ref, g2_ref, o_ref):
    a = _dot(ya_ref[...], wa_ref[...])
    b = _dot(yb_ref[...], wb_ref[...])
    c = _dot(yc_ref[...], wc_ref[...])
    for s in range(MERGE_TN // LANES):
        sl = slice(s * LANES, (s + 1) * LANES)
        o_ref[:, sl] = (g0_ref[s].astype(F32) * a[:, sl] + g1_ref[s].astype(F32) * b[:, sl]
                        + g2_ref[s].astype(F32) * c[:, sl]).astype(BF16)


def _merge(ya, yb, yc, wa, wb, wc, z, tm=1024):
    n = ya.shape[0]
    d = wa.shape[1]
    spt = MERGE_TN // LANES
    gate = lambda k: pl.BlockSpec((spt, tm, LANES), lambda i, j: (k * (d // MERGE_TN) + j, i, 0))
    act = lambda a: pl.BlockSpec((tm, a.shape[1]), lambda i, j: (i, 0))
    wgt = lambda w: pl.BlockSpec((w.shape[0], MERGE_TN), lambda i, j: (0, j))
    return pl.pallas_call(
        _merge_kernel,
        grid=(n // tm, d // MERGE_TN),
        in_specs=[act(ya), act(yb), act(yc), wgt(wa), wgt(wb), wgt(wc), gate(0), gate(1), gate(2)],
        out_specs=pl.BlockSpec((tm, MERGE_TN), lambda i, j: (i, j)),
        out_shape=jax.ShapeDtypeStruct((n, d), BF16),
        compiler_params=_cparams(("parallel", "arbitrary")),
    )(ya, yb, yc, wa, wb, wc, z, z, z)


def _out_proj_kernel(x_ref, m_ref, w_ref, o_ref):
    o_ref[...] = x_ref[...] + _dot(m_ref[...], w_ref[...])


def _out_proj(x2, merged, w_out, tm=1024, tn=512):
    n, d = x2.shape
    return pl.pallas_call(
        _out_proj_kernel,
        grid=(n // tm, d // tn),
        in_specs=[
            pl.BlockSpec((tm, tn), lambda i, j: (i, j)),
            pl.BlockSpec((tm, d), lambda i, j: (i, 0)),
            pl.BlockSpec((d, tn), lambda i, j: (0, j)),
        ],
        out_specs=pl.BlockSpec((tm, tn), lambda i, j: (i, j)),
        out_shape=jax.ShapeDtypeStruct((n, d), F32),
        compiler_params=_cparams(("parallel", "arbitrary")),
    )(x2, merged, w_out)


FFN_TF = 512


def _ffn_kernel(x_ref, g_ref, wg_ref, wu_ref, wd_ref, o_ref, h_ref):
    f = pl.program_id(1)

    @pl.when(f == 0)
    def _():
        x = x_ref[...]
        h_ref[...] = _rms(x, g_ref[...]).astype(BF16)
        o_ref[...] = x

    h = h_ref[...]
    a = (jax.nn.silu(_dot(h, wg_ref[...])) * _dot(h, wu_ref[...])).astype(BF16)
    o_ref[...] += _dot(a, wd_ref[...])


def _ffn(x2, g, w_gate, w_up, w_down, tm=512):
    n, d = x2.shape
    d_ff = w_gate.shape[1]
    return pl.pallas_call(
        _ffn_kernel,
        grid=(n // tm, d_ff // FFN_TF),
        in_specs=[
            pl.BlockSpec((tm, d), lambda i, f: (i, 0)),
            pl.BlockSpec((1, d), lambda i, f: (0, 0)),
            pl.BlockSpec((d, FFN_TF), lambda i, f: (0, f)),
            pl.BlockSpec((d, FFN_TF), lambda i, f: (0, f)),
            pl.BlockSpec((FFN_TF, d), lambda i, f: (f, 0)),
        ],
        out_specs=pl.BlockSpec((tm, d), lambda i, f: (i, 0)),
        out_shape=jax.ShapeDtypeStruct((n, d), F32),
        scratch_shapes=[pltpu.VMEM((tm, d), BF16)],
        compiler_params=_cparams(("parallel", "arbitrary")),
    )(x2, g, w_gate, w_up, w_down)


def _router_kernel(x_ref, g_ref, w_ref, b_ref, idx_ref, wt_ref):
    h = _rms(x_ref[...], g_ref[...])
    logits = jnp.dot(h, w_ref[...], precision=lax.Precision.HIGHEST, preferred_element_type=F32) + b_ref[...]
    lane = lax.broadcasted_iota(jnp.int32, logits.shape, 1)
    l1 = jnp.where(lane < N_EXPERTS, logits, -jnp.inf)
    m1 = jnp.max(l1, axis=-1, keepdims=True)
    i1 = jnp.min(jnp.where(l1 == m1, lane, LANES), axis=-1, keepdims=True)
    l2 = jnp.where(lane == i1, -jnp.inf, l1)
    m2 = jnp.max(l2, axis=-1, keepdims=True)
    i2 = jnp.min(jnp.where(l2 == m2, lane, LANES), axis=-1, keepdims=True)
    e = jnp.exp(m2 - m1)
    w1 = 1.0 / (1.0 + e)
    idx_ref[...] = jnp.where(lane == 0, i1, jnp.where(lane == 1, i2, 0))
    wt_ref[...] = jnp.where(lane == 0, w1, jnp.where(lane == 1, e * w1, 0.0))


def _router(x2, g, w_r, b_r, tm=512):
    n, d = x2.shape
    return pl.pallas_call(
        _router_kernel,
        grid=(n // tm,),
        in_specs=[
            pl.BlockSpec((tm, d), lambda i: (i, 0)),
            pl.BlockSpec((1, d), lambda i: (0, 0)),
            pl.BlockSpec((d, LANES), lambda i: (0, 0)),
            pl.BlockSpec((1, LANES), lambda i: (0, 0)),
        ],
        out_specs=[pl.BlockSpec((tm, LANES), lambda i: (i, 0))] * 2,
        out_shape=[jax.ShapeDtypeStruct((n, LANES), jnp.int32), jax.ShapeDtypeStruct((n, LANES), F32)],
        compiler_params=_cparams(("parallel",)),
    )(x2, g, w_r, b_r)


MOE_TM = 512
MOE_TF = 512


def _moe_kernel(tile_e_ref, n_used_ref, row_tok_ref, x_hbm, g_ref, wg_ref, wu_ref, wd_ref, o_ref,
                xbuf, h_ref, sem):
    t = pl.program_id(0)
    f = pl.program_id(1)
    tm = MOE_TM
    used = t < n_used_ref[0]

    def row_copy(r):
        tok = row_tok_ref[t * tm + r]
        return pltpu.make_async_copy(x_hbm.at[pl.ds(tok, 1), :], xbuf.at[pl.ds(r, 1), :], sem)

    @pl.when(f == 0)
    def _():
        o_ref[...] = jnp.zeros_like(o_ref)

    @pl.when((f == 0) & used)
    def _():
        def start(r, c):
            row_copy(r).start()
            return c

        def wait(r, c):
            row_copy(r).wait()
            return c

        lax.fori_loop(0, tm, start, 0, unroll=8)
        lax.fori_loop(0, tm, wait, 0, unroll=8)
        h_ref[...] = _rms(xbuf[...], g_ref[...]).astype(BF16)

    @pl.when(used)
    def _():
        h = h_ref[...]
        a = (jax.nn.silu(_dot(h, wg_ref[0])) * _dot(h, wu_ref[0])).astype(BF16)
        o_ref[...] += _dot(a, wd_ref[0])


def _moe_experts(x2, g, w_gate, w_up, w_down, tile_e, n_used, row_tok):
    n, d = x2.shape
    d_ff = w_gate.shape[2]
    n_f = d_ff // MOE_TF
    n_tiles = tile_e.shape[0]

    def f_eff(t, f, nu):
        return jnp.where(t < nu[0], f, n_f - 1)

    grid_spec = pltpu.PrefetchScalarGridSpec(
        num_scalar_prefetch=3,
        grid=(n_tiles, n_f),
        in_specs=[
            pl.BlockSpec(memory_space=pl.ANY),
            pl.BlockSpec((1, d), lambda t, f, te, nu, rt: (0, 0)),
            pl.BlockSpec((1, d, MOE_TF), lambda t, f, te, nu, rt: (te[t], 0, f_eff(t, f, nu))),
            pl.BlockSpec((1, d, MOE_TF), lambda t, f, te, nu, rt: (te[t], 0, f_eff(t, f, nu))),
            pl.BlockSpec((1, MOE_TF, d), lambda t, f, te, nu, rt: (te[t], f_eff(t, f, nu), 0)),
        ],
        out_specs=pl.BlockSpec((MOE_TM, d), lambda t, f, te, nu, rt: (t, 0)),
        scratch_shapes=[
            pltpu.VMEM((MOE_TM, d), F32),
            pltpu.VMEM((MOE_TM, d), BF16),
            pltpu.SemaphoreType.DMA,
        ],
    )
    return pl.pallas_call(
        _moe_kernel,
        grid_spec=grid_spec,
        out_shape=jax.ShapeDtypeStruct((n_tiles * MOE_TM, d), F32),
        compiler_params=_cparams(("arbitrary", "arbitrary")),
    )(tile_e, n_used, row_tok, x2, g, w_gate, w_up, w_down)


COMBINE_TM = 256


def _combine_kernel(pos_ref, x_ref, wt_ref, y_hbm, o_ref, buf, sem):
    i = pl.program_id(0)
    tm = COMBINE_TM

    def row_copy(r, k):
        src = pos_ref[(i * tm + r) * TOP_K + k]
        return pltpu.make_async_copy(y_hbm.at[pl.ds(src, 1), :], buf.at[k, pl.ds(r, 1), :], sem)

    def start(r, c):
        for k in range(TOP_K):
            row_copy(r, k).start()
        return c

    def wait(r, c):
        for k in range(TOP_K):
            row_copy(r, k).wait()
        return c

    lax.fori_loop(0, tm, start, 0, unroll=8)
    lax.fori_loop(0, tm, wait, 0, unroll=8)
    wt = wt_ref[...]
    acc = x_ref[...]
    for k in range(TOP_K):
        acc = acc + wt[:, k:k + 1] * buf[k]
    o_ref[...] = acc


def _combine(x2, wts, ys, pos):
    n, d = x2.shape
    grid_spec = pltpu.PrefetchScalarGridSpec(
        num_scalar_prefetch=1,
        grid=(n // COMBINE_TM,),
        in_specs=[
            pl.BlockSpec((COMBINE_TM, d), lambda i, p: (i, 0)),
            pl.BlockSpec((COMBINE_TM, LANES), lambda i, p: (i, 0)),
            pl.BlockSpec(memory_space=pl.ANY),
        ],
        out_specs=pl.BlockSpec((COMBINE_TM, d), lambda i, p: (i, 0)),
        scratch_shapes=[pltpu.VMEM((TOP_K, COMBINE_TM, d), F32), pltpu.SemaphoreType.DMA],
    )
    return pl.pallas_call(
        _combine_kernel,
        grid_spec=grid_spec,
        out_shape=jax.ShapeDtypeStruct((n, d), F32),
        compiler_params=_cparams(("arbitrary",)),
    )(pos, x2, wts, ys)


def _route_plan(idx, n):
    flat_e = idx[:, :TOP_K].reshape(-1)
    onehot = (flat_e[:, None] == jnp.arange(N_EXPERTS, dtype=jnp.int32)[None, :]).astype(jnp.int32)
    before = jnp.cumsum(onehot, axis=0) - onehot
    rank = jnp.sum(before * onehot, axis=1)
    counts = jnp.sum(onehot, axis=0)
    padded = ((counts + MOE_TM - 1) // MOE_TM) * MOE_TM
    ends = jnp.cumsum(padded)
    pos = (ends - padded)[flat_e] + rank
    n_rows = TOP_K * n + N_EXPERTS * MOE_TM
    row_tok = jnp.zeros((n_rows,), jnp.int32).at[pos].set(jnp.arange(TOP_K * n, dtype=jnp.int32) // TOP_K)
    tile_start = jnp.arange(n_rows // MOE_TM, dtype=jnp.int32) * MOE_TM
    tile_e = jnp.minimum(jnp.searchsorted(ends, tile_start, side="right"), N_EXPERTS - 1).astype(jnp.int32)
    n_used = (ends[-1] // MOE_TM).astype(jnp.int32).reshape(1)
    return tile_e, n_used, row_tok, pos.astype(jnp.int32)


def _moe(x2, g, w_r, b_r, w_gate, w_up, w_down):
    n = x2.shape[0]
    idx, wts = _router(x2, g, w_r, b_r)
    tile_e, n_used, row_tok, pos = _route_plan(idx, n)
    ys = _moe_experts(x2, g, w_gate, w_up, w_down, tile_e, n_used, row_tok)
    return _combine(x2, wts, ys, pos)


def _prep_in_proj(w_in_l):
    d_model = w_in_l.shape[0]
    q_w = NSA_HEADS * HEAD_DIM
    kv_w = NSA_BRANCHES * 2 * NSA_KV_GROUPS * HEAD_DIM
    gate_w = NSA_BRANCHES * NSA_HEADS
    dil_w = 3 * DIL_HEADS * HEAD_DIM
    pool_w = len(POOL_WINDOWS) * LANES
    o_q, o_kv, o_g = 0, q_w, q_w + kv_w
    o_dil = o_g + gate_w
    o_pool = o_dil + dil_w
    o_merge = o_pool + pool_w
    assert w_in_l.shape[1] == o_merge + 3 * d_model
    w_main = jnp.concatenate(
        [w_in_l[:, o_merge:], w_in_l[:, o_q:o_g], w_in_l[:, o_dil:o_merge]], axis=1).astype(BF16)
    assert w_main.shape[1] == N_SLABS * LANES
    cols = np.zeros((NSA_KV_GROUPS, LANES), np.int32)
    live = np.zeros((NSA_KV_GROUPS, LANES), np.float32)
    for g in range(NSA_KV_GROUPS):
        for br in range(NSA_BRANCHES):
            for hh in range(NSA_HPG):
                cols[g, br * NSA_HPG + hh] = o_g + br * NSA_HEADS + g * NSA_HPG + hh
                live[g, br * NSA_HPG + hh] = 1.0
    w_gate = (jnp.transpose(w_in_l[:, cols], (1, 0, 2)) * live[:, None, :]).astype(BF16)
    return w_main, w_gate


def _sel_tables(seq):
    n_cmp = seq // CMP_STRIDE - 1
    n_sel = seq // SEL_BLOCK
    i = np.arange(n_cmp)[:, None] * CMP_STRIDE
    j = np.arange(n_sel)[None, :] * SEL_BLOCK
    ov = np.clip(np.minimum(i + CMP_BLOCK, j + SEL_BLOCK) - np.maximum(i, j), 0, None) / CMP_BLOCK
    ovt = np.zeros((n_sel, n_cmp + 1), np.float32)
    ovt[:, :n_cmp] = ov.T
    esel = (np.arange(seq)[None, :] // SEL_BLOCK == np.arange(n_sel)[:, None]).astype(np.float32)
    return jnp.asarray(ovt), jnp.asarray(esel, dtype=BF16)


def kernel(x, mix_norm, w_in, nsa_q_norm, nsa_k_norm, cmp_w1, cmp_b1, cmp_w2, dil_q_norm, dil_k_norm, pool_w, pool_scale, w_branch_a, w_branch_b, w_branch_c, w_out, ffn_norm, ffn_w_gate, ffn_w_up, ffn_w_down, moe_w_router, moe_b_router, moe_w_gate, moe_w_up, moe_w_down):
    bsz, seq, d_model = x.shape
    depth = w_in.shape[0]
    n = bsz * seq
    nsa_slopes = _alibi_slopes(NSA_HEADS)
    dil_slopes = _alibi_slopes(DIL_HEADS)
    ovt, esel = _sel_tables(seq)
    x2 = x.reshape(n, d_model)
    for l in range(depth):
        w_main, w_gate = _prep_in_proj(w_in[l])
        z, gates = _in_proj(x2, mix_norm[l][None, :], w_main, w_gate)
        z4 = z.reshape(N_SLABS, bsz, seq, LANES)
        y_a = _nsa(z4, gates, nsa_q_norm[l][None, :], nsa_k_norm[l], cmp_w1[l].astype(BF16), cmp_b1[l],
                   cmp_w2[l].astype(BF16), nsa_slopes, ovt, esel)
        y_b = _dil(z4, dil_q_norm[l][None, :], dil_k_norm[l][None, :], dil_slopes)
        y_c = _pool(z4, pool_w[l].astype(BF16), pool_scale[l][None, :])
        merged = _merge(y_a, y_b, y_c, w_branch_a[l].astype(BF16), w_branch_b[l].astype(BF16),
                        w_branch_c[l].astype(BF16), z)
        x2 = _out_proj(x2, merged, w_out[l].astype(BF16))
        i = l // 2
        if l % 2 == 0:
            x2 = _ffn(x2, ffn_norm[l][None, :], ffn_w_gate[i].astype(BF16), ffn_w_up[i].astype(BF16),
                      ffn_w_down[i].astype(BF16))
        else:
            w_r = jnp.zeros((d_model, LANES), F32).at[:, :N_EXPERTS].set(moe_w_router[i])
            b_r = jnp.zeros((1, LANES), F32).at[0, :N_EXPERTS].set(moe_b_router[i])
            x2 = _moe(x2, ffn_norm[l][None, :], w_r, b_r, moe_w_gate[i].astype(BF16),
                      moe_w_up[i].astype(BF16), moe_w_down[i].astype(BF16))
    return x2.reshape(bsz, seq, d_model)
```

```python
import functools

import numpy as np
import jax
import jax.numpy as jnp
from jax import lax
from jax.experimental import pallas as pl
from jax.experimental.pallas import tpu as pltpu

F32 = jnp.float32
BF16 = jnp.bfloat16

HEAD_DIM = 128
NORM_EPS = 1e-6
NEG_INF = -1e30
NSA_HEADS = 6
NSA_KV_GROUPS = 2
NSA_HPG = NSA_HEADS // NSA_KV_GROUPS
NSA_BRANCHES = 3
CMP_BLOCK = 32
CMP_STRIDE = 16
SEL_BLOCK = 64
SEL_TOP = 8
SEL_FORCE = 1e4
WIN = 512
DIL_PATTERNS = ((128, 1), (512, 4), (2048, 16))
DIL_HPG = 2
DIL_HEADS = DIL_HPG * len(DIL_PATTERNS)
DIL_SPAN = 128
POOL_WINDOWS = (2, 4, 8, 16)
N_EXPERTS = 8
TOP_K = 2
SCALE = HEAD_DIM ** -0.5

LANES = 128
MERGE_SLABS = 3 * 16
Q_SLAB0 = MERGE_SLABS
KV_SLAB0 = Q_SLAB0 + NSA_HEADS
DIL_SLAB0 = KV_SLAB0 + NSA_BRANCHES * 2 * NSA_KV_GROUPS
POOL_SLAB0 = DIL_SLAB0 + 3 * DIL_HEADS
N_SLABS = POOL_SLAB0 + len(POOL_WINDOWS)

VMEM_LIMIT = 56 * 1024 * 1024


def _cparams(sem):
    return pltpu.CompilerParams(dimension_semantics=sem, vmem_limit_bytes=VMEM_LIMIT)


def _rms(x, g):
    return x * lax.rsqrt(jnp.mean(x * x, axis=-1, keepdims=True) + NORM_EPS) * g


def _dot(a, b):
    return jnp.dot(a, b, preferred_element_type=F32)


def _dot_nt(a, b):
    return lax.dot_general(a, b, (((1,), (1,)), ((), ())), preferred_element_type=F32)


def _alibi_slopes(n):
    return jnp.asarray(2.0 ** (-8.0 * np.arange(1, n + 1) / n), dtype=F32)


IN_TN = 1024
IN_SLABS_PER_TILE = IN_TN // LANES
IN_SIG_TILES = MERGE_SLABS // IN_SLABS_PER_TILE


def _in_proj_kernel(x_ref, g_ref, w_ref, wg_ref, z_ref, gate_ref, h_ref):
    j = pl.program_id(1)

    @pl.when(j == 0)
    def _():
        h = _rms(x_ref[...], g_ref[...]).astype(BF16)
        h_ref[...] = h
        for g in range(NSA_KV_GROUPS):
            gate_ref[g] = jax.nn.sigmoid(_dot(h, wg_ref[g]))

    acc = _dot(h_ref[...], w_ref[...])
    is_gate = j < IN_SIG_TILES
    for c in range(IN_SLABS_PER_TILE):
        a = acc[:, c * LANES:(c + 1) * LANES]
        z_ref[c] = jnp.where(is_gate, jax.nn.sigmoid(a), a).astype(BF16)


def _in_proj(x2, g, w_main, w_gate, layer, tm=512):
    n, d = x2.shape
    n_tiles = w_main.shape[2] // IN_TN
    return pl.pallas_call(
        _in_proj_kernel,
        grid=(n // tm, n_tiles),
        in_specs=[
            pl.BlockSpec((tm, d), lambda i, j: (i, 0)),
            pl.BlockSpec((1, d), lambda i, j: (0, 0)),
            pl.BlockSpec((None, d, IN_TN), lambda i, j: (layer, 0, j)),
            pl.BlockSpec((None, NSA_KV_GROUPS, d, LANES), lambda i, j: (layer, 0, 0, 0)),
        ],
        out_specs=[
            pl.BlockSpec((IN_SLABS_PER_TILE, tm, LANES), lambda i, j: (j, i, 0)),
            pl.BlockSpec((NSA_KV_GROUPS, tm, LANES), lambda i, j: (0, i, 0)),
        ],
        out_shape=[
            jax.ShapeDtypeStruct((N_SLABS, n, LANES), BF16),
            jax.ShapeDtypeStruct((NSA_KV_GROUPS, n, LANES), F32),
        ],
        scratch_shapes=[pltpu.VMEM((tm, d), BF16)],
        compiler_params=_cparams(("parallel", "arbitrary")),
    )(x2, g, w_main, w_gate)


NSA_TQ = 256
LOG2E = 1.4426950408889634
NSA_POS_SPLIT = 64
NSA_SEL_LANE0 = 8
NSA_BIG = 2.0 ** 100


def _key_table(pos, block=None):
    tab = np.zeros((len(pos), LANES), np.float32)
    tab[:, 0:3] = (pos // NSA_POS_SPLIT * NSA_POS_SPLIT)[:, None]
    tab[:, 3:6] = (pos % NSA_POS_SPLIT)[:, None]
    if block is not None:
        tab[np.arange(len(pos)), NSA_SEL_LANE0 + pos // block] = 1.0
    return tab


def _slope_row(slope):
    lane = lax.broadcasted_iota(jnp.int32, (1, LANES), 1)
    sv = jnp.full((1, LANES), slope, F32)
    hi = sv.astype(BF16).astype(F32)
    mid = (sv - hi).astype(BF16).astype(F32)
    lo = sv - hi - mid
    part = jnp.where((lane == 0) | (lane == 3), hi, jnp.where((lane == 1) | (lane == 4), mid, lo))
    return jnp.where(lane < 6, part, 0.0)


def _nsa_kernel(slope_ref, q_ref, kc_ref, vc_ref, ks_ref, vs_ref, kw_ref, vw_ref, gate_ref,
                qg_ref, kg_ref, w1_ref, b1_ref, w2_ref, ovt_ref, ktab_ref, ctab_ref, o_ref,
                tmp32, kca_s, vcn_s, ksa_s, kwa_s, qa_s, osel_s):
    g = pl.program_id(1)
    qi = pl.program_id(2)
    tq = NSA_TQ
    seq = ks_ref.shape[2]
    n_cmp_pad = seq // CMP_STRIDE
    n_sel = seq // SEL_BLOCK
    hd = HEAD_DIM

    @pl.when(qi == 0)
    def _():
        for which, src in enumerate((kc_ref, vc_ref)):
            tmp32[...] = src[0, 0].astype(F32)
            lo = jnp.zeros((n_cmp_pad, HEAD_DIM), F32)
            hi = jnp.zeros((n_cmp_pad, HEAD_DIM), F32)
            for l in range(CMP_STRIDE):
                xl = tmp32[pl.ds(l, n_cmp_pad, stride=CMP_STRIDE), :].astype(BF16)
                lo = lo + _dot(xl, w1_ref[which, l])
                hi = hi + _dot(xl, w1_ref[which, CMP_STRIDE + l])
            hid = jax.nn.gelu(lo + pltpu.roll(hi, n_cmp_pad - 1, axis=0) + b1_ref[which:which + 1, :])
            c = _dot(hid.astype(BF16), w2_ref[which])
            if which == 0:
                kca_s[:, :hd] = _rms(c, kg_ref[0:1, :]).astype(BF16)
                kca_s[:, hd:] = ctab_ref[...]
            else:
                vcn_s[...] = c.astype(BF16)
        ksa_s[:, :hd] = _rms(ks_ref[0, 0].astype(F32), kg_ref[1:2, :]).astype(BF16)
        ksa_s[:, hd:] = ktab_ref[...]
        kwa_s[:, :hd] = _rms(kw_ref[0, 0].astype(F32), kg_ref[2:3, :]).astype(BF16)
        kwa_s[:, hd:] = ktab_ref[...]

    q0 = qi * tq
    t_col = q0 + lax.broadcasted_iota(jnp.int32, (tq, 1), 0)
    rows_of = lambda hh: slice(hh * tq, (hh + 1) * tq)
    slope_rows = []
    for hh in range(NSA_HPG):
        qs = (_rms(q_ref[hh, 0].astype(F32), qg_ref[...]) * (SCALE * LOG2E)).astype(BF16)
        slope_rows.append(_slope_row(slope_ref[g * NSA_HPG + hh] * LOG2E))
        qa_s[0, rows_of(hh), :hd] = qs
        qa_s[1, rows_of(hh), :hd] = qs
        qa_s[0, rows_of(hh), hd:] = jnp.broadcast_to(slope_rows[hh].astype(BF16), (tq, LANES))

    def stack(x):
        return jnp.concatenate([x] * NSA_HPG, axis=0)

    n_row = lax.broadcasted_iota(jnp.int32, (1, n_cmp_pad), 1)
    c_end = n_row * CMP_STRIDE + (CMP_BLOCK - 1)
    valid_c = (c_end <= t_col) & (n_row < n_cmp_pad - 1)
    s = _dot_nt(qa_s[0], kca_s[...]) + stack(jnp.where(valid_c, 0.0, NEG_INF))
    e = jnp.exp2(s - jnp.max(s, axis=-1, keepdims=True))
    p = e * (stack(valid_c.astype(F32)) / jnp.sum(e, axis=-1, keepdims=True))
    o_cmp = _dot(p.astype(BF16), vcn_s[...])
    psum = functools.reduce(lambda a, b: a + b, [p[rows_of(hh)] for hh in range(NSA_HPG)])

    imp_t = lax.dot_general(ovt_ref[...], psum, (((1,), (1,)), ((), ())),
                            precision=lax.Precision.HIGHEST, preferred_element_type=F32)
    blk = lax.broadcasted_iota(jnp.int32, (n_sel, tq), 0)
    cur = lax.shift_right_logical(q0 + lax.broadcasted_iota(jnp.int32, (n_sel, tq), 1),
                                  int(np.log2(SEL_BLOCK)))
    forced = (blk == 0) | (blk == cur) | (blk == cur - 1)
    imp = jnp.where(forced, SEL_FORCE, jnp.where(blk <= cur, imp_t, -1.0))
    rank = jnp.zeros((n_sel, tq), jnp.int32)
    for i in range(n_sel):
        ri = imp[i:i + 1, :]
        beats = (ri > imp) | ((ri == imp) & (blk > i))
        rank = rank + beats.astype(jnp.int32)
    sel_lanes = jnp.where(rank < min(SEL_TOP, n_sel), 0.0, -NSA_BIG)
    sel_lanes = jnp.concatenate(
        [jnp.zeros((NSA_SEL_LANE0, tq), F32), sel_lanes,
         jnp.zeros((LANES - NSA_SEL_LANE0 - n_sel, tq), F32)], axis=0).T
    for hh in range(NSA_HPG):
        qa_s[1, rows_of(hh), hd:] = (slope_rows[hh] + sel_lanes).astype(BF16)

    def attend(q, k, v, mask):
        s = _dot_nt(q, k) + mask
        e = jnp.exp2(s - jnp.max(s, axis=-1, keepdims=True))
        return _dot(e.astype(BF16), v) / jnp.sum(e, axis=-1, keepdims=True)

    k_pos = lax.broadcasted_iota(jnp.int32, (1, seq), 1)
    s_mask = jnp.where(k_pos <= t_col, 0.0, NEG_INF)

    w_len = WIN + tq
    w0 = pl.multiple_of(jnp.maximum(q0 - WIN, 0), tq)
    w_dist = t_col - (w0 + lax.broadcasted_iota(jnp.int32, (1, w_len), 1))
    w_mask = jnp.where(w_dist >= 0, jnp.where(w_dist < WIN, 0.0, NEG_INF), NEG_INF)
    kwin = kwa_s[pl.ds(w0, w_len), :]
    vwin = vw_ref[0, 0, pl.ds(w0, w_len), :]

    nq = seq // tq
    for lo, hi in ((0, nq // 2), (nq // 2, nq)):
        @pl.when((qi >= lo) & (qi < hi))
        def _(nk=hi * tq):
            for hh in range(NSA_HPG):
                osel_s[hh] = attend(qa_s[1, rows_of(hh)], ksa_s[:nk, :], vs_ref[0, 0, :nk, :], s_mask[:, :nk])

    gates = gate_ref[0]
    for hh in range(NSA_HPG):
        o_win = attend(qa_s[0, rows_of(hh)], kwin, vwin, w_mask)
        out = (gates[:, hh:hh + 1] * o_cmp[rows_of(hh)]
               + gates[:, NSA_HPG + hh:NSA_HPG + hh + 1] * osel_s[hh]
               + gates[:, 2 * NSA_HPG + hh:2 * NSA_HPG + hh + 1] * o_win)
        o_ref[:, hh * HEAD_DIM:(hh + 1) * HEAD_DIM] = out.astype(BF16)


def _nsa(z4, gates, q_gain, k_gain, w1, b1, w2, slopes, ovt, ktab, ctab):
    _, bsz, seq, _ = z4.shape
    tq = NSA_TQ
    nq = seq // tq
    n = bsz * seq

    def kv_spec(br, kvi):
        base = KV_SLAB0 + br * 2 * NSA_KV_GROUPS + kvi * NSA_KV_GROUPS
        return pl.BlockSpec((1, 1, seq, HEAD_DIM), lambda b, g, qi: (base + g, b, 0, 0))

    full = lambda shape: pl.BlockSpec(shape, lambda b, g, qi: (0,) * len(shape))
    return pl.pallas_call(
        _nsa_kernel,
        grid=(bsz, NSA_KV_GROUPS, nq),
        in_specs=[
            pl.BlockSpec(memory_space=pltpu.SMEM),
            pl.BlockSpec((NSA_HPG, 1, tq, HEAD_DIM), lambda b, g, qi: (Q_SLAB0 // NSA_HPG + g, b, qi, 0)),
            kv_spec(0, 0), kv_spec(0, 1), kv_spec(1, 0), kv_spec(1, 1), kv_spec(2, 0), kv_spec(2, 1),
            pl.BlockSpec((1, tq, LANES), lambda b, g, qi: (g, b * nq + qi, 0)),
            full((1, HEAD_DIM)), full((NSA_BRANCHES, HEAD_DIM)),
            full((2, CMP_BLOCK, HEAD_DIM, HEAD_DIM)), full((2, HEAD_DIM)), full((2, HEAD_DIM, HEAD_DIM)),
            full(ovt.shape), full(ktab.shape), full(ctab.shape),
        ],
        out_specs=pl.BlockSpec((tq, NSA_HPG * HEAD_DIM), lambda b, g, qi: (b * nq + qi, g)),
        out_shape=jax.ShapeDtypeStruct((n, NSA_HEADS * HEAD_DIM), BF16),
        scratch_shapes=[
            pltpu.VMEM((seq, HEAD_DIM), F32),
            pltpu.VMEM((seq // CMP_STRIDE, HEAD_DIM + LANES), BF16),
            pltpu.VMEM((seq // CMP_STRIDE, HEAD_DIM), BF16),
            pltpu.VMEM((seq, HEAD_DIM + LANES), BF16),
            pltpu.VMEM((seq, HEAD_DIM + LANES), BF16),
            pltpu.VMEM((2, NSA_HPG * tq, HEAD_DIM + LANES), BF16),
            pltpu.VMEM((NSA_HPG, tq, HEAD_DIM), F32),
        ],
        compiler_params=_cparams(("parallel", "parallel", "arbitrary")),
    )(slopes, z4, z4, z4, z4, z4, z4, z4, gates, q_gain, k_gain, w1, b1, w2, ovt, ktab, ctab)


DIL_TQ = 256


def _dil_kernel(slope_ref, *refs):
    n_pat = len(DIL_PATTERNS)
    qkv_refs = refs[:3 * n_pat]
    qg_ref, kg_ref, o_ref, qs, ks, vs, num_s, m_s, den_s = refs[3 * n_pat:]
    h = pl.program_id(1)
    seq = qs.shape[0]

    for p, (w, r) in enumerate(DIL_PATTERNS):
        assert w // r == DIL_SPAN
        q_ref, k_ref, v_ref = qkv_refs[3 * p:3 * p + 3]
        qs[...] = _rms(q_ref[0, 0].astype(F32), qg_ref[...])
        ks[...] = _rms(k_ref[0, 0].astype(F32), kg_ref[...])
        vs[...] = v_ref[0, 0].astype(F32)
        slope_r = slope_ref[DIL_HPG * p + h] * float(r)
        length = seq // r
        tq = min(DIL_TQ, length)
        nk = min(length, tq + DIL_SPAN)

        def rows(start, size, r=r):
            return pl.ds(start, size) if r == 1 else pl.ds(start, size, stride=r)

        def tile(idx, carry, p=p, r=r, tq=tq, nk=nk, length=length, slope_r=slope_r, rows=rows):
            c = idx // (length // tq)
            ti = idx % (length // tq)
            u0 = ti * tq
            k0 = jnp.maximum(u0 - DIL_SPAN, 0)
            rq = rows(c + u0 * r, tq)
            rk = rows(c + k0 * r, nk)
            q = qs[rq, :].astype(BF16)
            k = ks[rk, :].astype(BF16)
            v = vs[rk, :].astype(BF16)
            du = (u0 + lax.broadcasted_iota(jnp.int32, (tq, 1), 0)) - (k0 + lax.broadcasted_iota(jnp.int32, (1, nk), 1))
            s = _dot_nt(q, k) * SCALE - slope_r * du.astype(F32)
            s = jnp.where(du >= 0, jnp.where(du <= DIL_SPAN, s, NEG_INF), NEG_INF)
            m = jnp.max(s, axis=-1, keepdims=True)
            e = jnp.exp(s - m)
            den = jnp.sum(e, axis=-1, keepdims=True)
            num_s[p, rq, :] = _dot(e.astype(BF16), v)
            m_s[p, rq, :] = jnp.broadcast_to(m, (tq, HEAD_DIM))
            den_s[p, rq, :] = jnp.broadcast_to(den, (tq, HEAD_DIM))
            return carry

        lax.fori_loop(0, r * (length // tq), tile, 0)

    m_all = [m_s[p] for p in range(n_pat)]
    m_max = functools.reduce(jnp.maximum, m_all)
    numer = jnp.zeros((seq, HEAD_DIM), F32)
    denom = jnp.zeros((seq, HEAD_DIM), F32)
    for p in range(n_pat):
        wt = jnp.exp(m_all[p] - m_max)
        numer = numer + wt * num_s[p]
        denom = denom + wt * den_s[p]
    o_ref[...] = (numer / denom).astype(BF16)


def _dil(z4, q_gain, k_gain, slopes):
    _, bsz, seq, _ = z4.shape
    n = bsz * seq
    n_pat = len(DIL_PATTERNS)

    def spec(which, p):
        base = DIL_SLAB0 + which * DIL_HEADS + DIL_HPG * p
        return pl.BlockSpec((1, 1, seq, HEAD_DIM), lambda b, h: (base + h, b, 0, 0))

    qkv_specs = [spec(which, p) for p in range(n_pat) for which in range(3)]
    gain = pl.BlockSpec((1, HEAD_DIM), lambda b, h: (0, 0))
    return pl.pallas_call(
        _dil_kernel,
        grid=(bsz, DIL_HPG),
        in_specs=[pl.BlockSpec(memory_space=pltpu.SMEM)] + qkv_specs + [gain, gain],
        out_specs=pl.BlockSpec((seq, HEAD_DIM), lambda b, h: (b, h)),
        out_shape=jax.ShapeDtypeStruct((n, DIL_HPG * HEAD_DIM), BF16),
        scratch_shapes=[pltpu.VMEM((seq, HEAD_DIM), F32)] * 3
        + [pltpu.VMEM((n_pat, seq, HEAD_DIM), F32)] * 3,
        compiler_params=_cparams(("parallel", "parallel")),
    )(slopes, *([z4] * (3 * n_pat)), q_gain, k_gain)


def _pool_kernel(u_ref, w_ref, sc_ref, o_ref):
    seq = u_ref.shape[2]
    t = lax.broadcasted_iota(jnp.int32, (seq, 1), 0)
    for g, w in enumerate(POOL_WINDOWS):
        u = u_ref[g, 0].astype(F32)
        acc = u
        span = 1
        while span < w:
            acc = acc + jnp.where(t >= span, pltpu.roll(acc, span, axis=0), 0.0)
            span *= 2
        d = acc / jnp.minimum(t + 1, w).astype(F32) - u
        y = _dot(d.astype(BF16), w_ref[g]) * sc_ref[:, g * LANES:(g + 1) * LANES]
        o_ref[:, g * LANES:(g + 1) * LANES] = y.astype(BF16)


def _pool(z4, w_pool, scale):
    _, bsz, seq, _ = z4.shape
    ng = len(POOL_WINDOWS)
    assert all(w & (w - 1) == 0 for w in POOL_WINDOWS) and POOL_SLAB0 % ng == 0
    return pl.pallas_call(
        _pool_kernel,
        grid=(bsz,),
        in_specs=[
            pl.BlockSpec((ng, 1, seq, LANES), lambda b: (POOL_SLAB0 // ng, b, 0, 0)),
            pl.BlockSpec((ng, LANES, LANES), lambda b: (0, 0, 0)),
            pl.BlockSpec((1, ng * LANES), lambda b: (0, 0)),
        ],
        out_specs=pl.BlockSpec((seq, ng * LANES), lambda b: (b, 0)),
        out_shape=jax.ShapeDtypeStruct((bsz * seq, ng * LANES), BF16),
        compiler_params=_cparams(("parallel",)),
    )(z4, w_pool, scale)


MERGE_TN = 512


def _merge_kernel(ya_ref, yb_ref, yc_ref, wa_ref, wb_ref, wc_ref, g0_ref, g1_ref, g2_ref, o_ref):
    a = _dot(ya_ref[...], wa_ref[...])
    b = _dot(yb_ref[...], wb_ref[...])
    c = _dot(yc_ref[...], wc_ref[...])
    for s in range(MERGE_TN // LANES):
        sl = slice(s * LANES, (s + 1) * LANES)
        o_ref[:, sl] = (g0_ref[s].astype(F32) * a[:, sl] + g1_ref[s].astype(F32) * b[:, sl]
                        + g2_ref[s].astype(F32) * c[:, sl]).astype(BF16)


def _merge(ya, yb, yc, wa, wb, wc, z, layer, tm=1024):
    n = ya.shape[0]
    d = wa.shape[2]
    spt = MERGE_TN // LANES
    gate = lambda k: pl.BlockSpec((spt, tm, LANES), lambda i, j: (k * (d // MERGE_TN) + j, i, 0))
    act = lambda a: pl.BlockSpec((tm, a.shape[1]), lambda i, j: (i, 0))
    wgt = lambda w: pl.BlockSpec((None, w.shape[1], MERGE_TN), lambda i, j: (layer, 0, j))
    return pl.pallas_call(
        _merge_kernel,
        grid=(n // tm, d // MERGE_TN),
        in_specs=[act(ya), act(yb), act(yc), wgt(wa), wgt(wb), wgt(wc), gate(0), gate(1), gate(2)],
        out_specs=pl.BlockSpec((tm, MERGE_TN), lambda i, j: (i, j)),
        out_shape=jax.ShapeDtypeStruct((n, d), BF16),
        compiler_params=_cparams(("parallel", "arbitrary")),
    )(ya, yb, yc, wa, wb, wc, z, z, z)


def _out_proj_kernel(x_ref, m_ref, w_ref, o_ref):
    o_ref[...] = x_ref[...] + _dot(m_ref[...], w_ref[...])


def _out_proj(x2, merged, w_out, layer, tm=1024, tn=512):
    n, d = x2.shape
    return pl.pallas_call(
        _out_proj_kernel,
        grid=(n // tm, d // tn),
        in_specs=[
            pl.BlockSpec((tm, tn), lambda i, j: (i, j)),
            pl.BlockSpec((tm, d), lambda i, j: (i, 0)),
            pl.BlockSpec((None, d, tn), lambda i, j: (layer, 0, j)),
        ],
        out_specs=pl.BlockSpec((tm, tn), lambda i, j: (i, j)),
        out_shape=jax.ShapeDtypeStruct((n, d), F32),
        compiler_params=_cparams(("parallel", "arbitrary")),
    )(x2, merged, w_out)


FFN_TF = 512


def _ffn_kernel(x_ref, g_ref, wg_ref, wu_ref, wd_ref, o_ref, h_ref):
    f = pl.program_id(1)

    @pl.when(f == 0)
    def _():
        x = x_ref[...]
        h_ref[...] = _rms(x, g_ref[...]).astype(BF16)
        o_ref[...] = x

    h = h_ref[...]
    a = (jax.nn.silu(_dot(h, wg_ref[...])) * _dot(h, wu_ref[...])).astype(BF16)
    o_ref[...] += _dot(a, wd_ref[...])


def _ffn(x2, g, w_gate, w_up, w_down, layer, tm=512):
    n, d = x2.shape
    d_ff = w_gate.shape[2]
    return pl.pallas_call(
        _ffn_kernel,
        grid=(n // tm, d_ff // FFN_TF),
        in_specs=[
            pl.BlockSpec((tm, d), lambda i, f: (i, 0)),
            pl.BlockSpec((1, d), lambda i, f: (0, 0)),
            pl.BlockSpec((None, d, FFN_TF), lambda i, f: (layer, 0, f)),
            pl.BlockSpec((None, d, FFN_TF), lambda i, f: (layer, 0, f)),
            pl.BlockSpec((None, FFN_TF, d), lambda i, f: (layer, f, 0)),
        ],
        out_specs=pl.BlockSpec((tm, d), lambda i, f: (i, 0)),
        out_shape=jax.ShapeDtypeStruct((n, d), F32),
        scratch_shapes=[pltpu.VMEM((tm, d), BF16)],
        compiler_params=_cparams(("parallel", "arbitrary")),
    )(x2, g, w_gate, w_up, w_down)


def _router_kernel(x_ref, g_ref, w_ref, b_ref, idx_ref, wt_ref):
    h = _rms(x_ref[...], g_ref[...])
    logits = jnp.dot(h, w_ref[...], precision=lax.Precision.HIGHEST, preferred_element_type=F32) + b_ref[...]
    lane = lax.broadcasted_iota(jnp.int32, logits.shape, 1)
    l1 = jnp.where(lane < N_EXPERTS, logits, -jnp.inf)
    m1 = jnp.max(l1, axis=-1, keepdims=True)
    i1 = jnp.min(jnp.where(l1 == m1, lane, LANES), axis=-1, keepdims=True)
    l2 = jnp.where(lane == i1, -jnp.inf, l1)
    m2 = jnp.max(l2, axis=-1, keepdims=True)
    i2 = jnp.min(jnp.where(l2 == m2, lane, LANES), axis=-1, keepdims=True)
    e = jnp.exp(m2 - m1)
    w1 = 1.0 / (1.0 + e)
    idx_ref[...] = jnp.where(lane == 0, i1, jnp.where(lane == 1, i2, 0))
    wt_ref[...] = jnp.where(lane == 0, w1, jnp.where(lane == 1, e * w1, 0.0))


def _router(x2, g, w_r, b_r, tm=512):
    n, d = x2.shape
    return pl.pallas_call(
        _router_kernel,
        grid=(n // tm,),
        in_specs=[
            pl.BlockSpec((tm, d), lambda i: (i, 0)),
            pl.BlockSpec((1, d), lambda i: (0, 0)),
            pl.BlockSpec((d, LANES), lambda i: (0, 0)),
            pl.BlockSpec((1, LANES), lambda i: (0, 0)),
        ],
        out_specs=[pl.BlockSpec((tm, LANES), lambda i: (i, 0))] * 2,
        out_shape=[jax.ShapeDtypeStruct((n, LANES), jnp.int32), jax.ShapeDtypeStruct((n, LANES), F32)],
        compiler_params=_cparams(("parallel",)),
    )(x2, g, w_r, b_r)


MOE_TM = 512
MOE_TF = 512


def _moe_kernel(tile_e_ref, n_used_ref, row_tok_ref, x_hbm, g_ref, wg_ref, wu_ref, wd_ref, o_ref,
                xbuf, h_ref, sem):
    t = pl.program_id(0)
    f = pl.program_id(1)
    tm = MOE_TM
    n_used = n_used_ref[0]
    used = t < n_used
    slot = lax.rem(t, 2)

    def row_copy(tile, r, s):
        tok = row_tok_ref[tile * tm + r]
        return pltpu.make_async_copy(x_hbm.at[pl.ds(tok, 1), :], xbuf.at[s, pl.ds(r, 1), :], sem.at[s])

    def start_tile(tile, s):
        lax.fori_loop(0, tm, lambda r, c: (row_copy(tile, r, s).start(), c)[1], 0, unroll=8)

    def wait_tile(tile, s):
        lax.fori_loop(0, tm, lambda r, c: (row_copy(tile, r, s).wait(), c)[1], 0, unroll=8)

    @pl.when(f == 0)
    def _():
        o_ref[...] = jnp.zeros_like(o_ref)

    @pl.when((f == 0) & (t == 0))
    def _():
        start_tile(0, 0)

    @pl.when((f == 0) & used)
    def _():
        wait_tile(t, slot)

        @pl.when(t + 1 < n_used)
        def _():
            start_tile(t + 1, 1 - slot)

        h_ref[...] = _rms(xbuf[slot], g_ref[...]).astype(BF16)

    @pl.when(used)
    def _():
        h = h_ref[...]
        a = (jax.nn.silu(_dot(h, wg_ref[...])) * _dot(h, wu_ref[...])).astype(BF16)
        o_ref[...] += _dot(a, wd_ref[...])


def _moe_experts(x2, g, w_gate, w_up, w_down, layer, tile_e, n_used, row_tok):
    n, d = x2.shape
    d_ff = w_gate.shape[2]
    n_f = d_ff // MOE_TF
    n_tiles = tile_e.shape[0]
    e0 = layer * N_EXPERTS

    def f_eff(t, f, nu):
        return jnp.where(t < nu[0], f, n_f - 1)

    grid_spec = pltpu.PrefetchScalarGridSpec(
        num_scalar_prefetch=3,
        grid=(n_tiles, n_f),
        in_specs=[
            pl.BlockSpec(memory_space=pl.ANY),
            pl.BlockSpec((1, d), lambda t, f, te, nu, rt: (0, 0)),
            pl.BlockSpec((None, d, MOE_TF), lambda t, f, te, nu, rt: (e0 + te[t], 0, f_eff(t, f, nu))),
            pl.BlockSpec((None, d, MOE_TF), lambda t, f, te, nu, rt: (e0 + te[t], 0, f_eff(t, f, nu))),
            pl.BlockSpec((None, MOE_TF, d), lambda t, f, te, nu, rt: (e0 + te[t], f_eff(t, f, nu), 0)),
        ],
        out_specs=pl.BlockSpec((MOE_TM, d), lambda t, f, te, nu, rt: (t, 0)),
        scratch_shapes=[
            pltpu.VMEM((2, MOE_TM, d), F32),
            pltpu.VMEM((MOE_TM, d), BF16),
            pltpu.SemaphoreType.DMA((2,)),
        ],
    )
    return pl.pallas_call(
        _moe_kernel,
        grid_spec=grid_spec,
        out_shape=jax.ShapeDtypeStruct((n_tiles * MOE_TM, d), F32),
        compiler_params=_cparams(("arbitrary", "arbitrary")),
    )(tile_e, n_used, row_tok, x2, g, w_gate, w_up, w_down)


COMBINE_TM = 256


def _combine_kernel(pos_ref, x_ref, wt_ref, y_hbm, o_ref, buf, sem):
    i = pl.program_id(0)
    tm = COMBINE_TM
    slot = lax.rem(i, 2)

    def row_copy(tile, r, k, s):
        src = pos_ref[(tile * tm + r) * TOP_K + k]
        return pltpu.make_async_copy(y_hbm.at[pl.ds(src, 1), :], buf.at[s, k, pl.ds(r, 1), :], sem.at[s])

    def start_tile(tile, s):
        def body(r, c):
            for k in range(TOP_K):
                row_copy(tile, r, k, s).start()
            return c
        lax.fori_loop(0, tm, body, 0, unroll=8)

    def wait_tile(tile, s):
        def body(r, c):
            for k in range(TOP_K):
                row_copy(tile, r, k, s).wait()
            return c
        lax.fori_loop(0, tm, body, 0, unroll=8)

    @pl.when(i == 0)
    def _():
        start_tile(0, 0)

    wait_tile(i, slot)

    @pl.when(i + 1 < pl.num_programs(0))
    def _():
        start_tile(i + 1, 1 - slot)

    wt = wt_ref[...]
    acc = x_ref[...]
    for k in range(TOP_K):
        acc = acc + wt[:, k:k + 1] * buf[slot, k]
    o_ref[...] = acc


def _combine(x2, wts, ys, pos):
    n, d = x2.shape
    grid_spec = pltpu.PrefetchScalarGridSpec(
        num_scalar_prefetch=1,
        grid=(n // COMBINE_TM,),
        in_specs=[
            pl.BlockSpec((COMBINE_TM, d), lambda i, p: (i, 0)),
            pl.BlockSpec((COMBINE_TM, LANES), lambda i, p: (i, 0)),
            pl.BlockSpec(memory_space=pl.ANY),
        ],
        out_specs=pl.BlockSpec((COMBINE_TM, d), lambda i, p: (i, 0)),
        scratch_shapes=[pltpu.VMEM((2, TOP_K, COMBINE_TM, d), F32), pltpu.SemaphoreType.DMA((2,))],
    )
    return pl.pallas_call(
        _combine_kernel,
        grid_spec=grid_spec,
        out_shape=jax.ShapeDtypeStruct((n, d), F32),
        compiler_params=_cparams(("arbitrary",)),
    )(pos, x2, wts, ys)


def _route_plan(idx, n):
    flat_e = idx[:, :TOP_K].reshape(-1)
    onehot = (flat_e[:, None] == jnp.arange(N_EXPERTS, dtype=jnp.int32)[None, :]).astype(jnp.int32)
    before = jnp.cumsum(onehot, axis=0) - onehot
    rank = jnp.sum(before * onehot, axis=1)
    counts = jnp.sum(onehot, axis=0)
    padded = ((counts + MOE_TM - 1) // MOE_TM) * MOE_TM
    ends = jnp.cumsum(padded)
    pos = (ends - padded)[flat_e] + rank
    n_rows = TOP_K * n + N_EXPERTS * MOE_TM
    row_tok = jnp.zeros((n_rows,), jnp.int32).at[pos].set(jnp.arange(TOP_K * n, dtype=jnp.int32) // TOP_K)
    tile_start = jnp.arange(n_rows // MOE_TM, dtype=jnp.int32) * MOE_TM
    tile_e = jnp.sum((tile_start[:, None] >= ends[None, :]).astype(jnp.int32), axis=1)
    tile_e = jnp.minimum(tile_e, N_EXPERTS - 1)
    n_used = (ends[-1] // MOE_TM).astype(jnp.int32).reshape(1)
    return tile_e, n_used, row_tok, pos.astype(jnp.int32)


def _moe(x2, g, w_r, b_r, w_gate, w_up, w_down, layer):
    n = x2.shape[0]
    idx, wts = _router(x2, g, w_r, b_r)
    tile_e, n_used, row_tok, pos = _route_plan(idx, n)
    ys = _moe_experts(x2, g, w_gate, w_up, w_down, layer, tile_e, n_used, row_tok)
    return _combine(x2, wts, ys, pos)


def _prep_in_proj(w_in):
    d_model = w_in.shape[1]
    q_w = NSA_HEADS * HEAD_DIM
    kv_w = NSA_BRANCHES * 2 * NSA_KV_GROUPS * HEAD_DIM
    gate_w = NSA_BRANCHES * NSA_HEADS
    dil_w = 3 * DIL_HEADS * HEAD_DIM
    pool_w = len(POOL_WINDOWS) * LANES
    o_q, o_kv, o_g = 0, q_w, q_w + kv_w
    o_dil = o_g + gate_w
    o_pool = o_dil + dil_w
    o_merge = o_pool + pool_w
    assert w_in.shape[2] == o_merge + 3 * d_model
    w_bf = w_in.astype(BF16)
    w_main = jnp.concatenate(
        [w_bf[:, :, o_merge:], w_bf[:, :, o_q:o_g], w_bf[:, :, o_dil:o_merge]], axis=2)
    assert w_main.shape[2] == N_SLABS * LANES
    groups = []
    for g in range(NSA_KV_GROUPS):
        parts = []
        for br in range(NSA_BRANCHES):
            c0 = o_g + br * NSA_HEADS + g * NSA_HPG
            parts.append(w_bf[:, :, c0:c0 + NSA_HPG])
        parts.append(jnp.zeros(w_bf.shape[:2] + (LANES - NSA_BRANCHES * NSA_HPG,), BF16))
        groups.append(jnp.concatenate(parts, axis=2))
    w_gate = jnp.stack(groups, axis=1)
    return w_main, w_gate


def _sel_tables(seq):
    n_cmp = seq // CMP_STRIDE - 1
    n_sel = seq // SEL_BLOCK
    i = np.arange(n_cmp)[:, None] * CMP_STRIDE
    j = np.arange(n_sel)[None, :] * SEL_BLOCK
    ov = np.clip(np.minimum(i + CMP_BLOCK, j + SEL_BLOCK) - np.maximum(i, j), 0, None) / CMP_BLOCK
    ovt = np.zeros((n_sel, n_cmp + 1), np.float32)
    ovt[:, :n_cmp] = ov.T
    assert NSA_SEL_LANE0 + n_sel <= LANES
    ktab = _key_table(np.arange(seq), SEL_BLOCK)
    ctab = _key_table(np.arange(n_cmp + 1) * CMP_STRIDE + CMP_BLOCK - 1)
    return jnp.asarray(ovt), jnp.asarray(ktab, dtype=BF16), jnp.asarray(ctab, dtype=BF16)


def kernel(x, mix_norm, w_in, nsa_q_norm, nsa_k_norm, cmp_w1, cmp_b1, cmp_w2, dil_q_norm, dil_k_norm, pool_w, pool_scale, w_branch_a, w_branch_b, w_branch_c, w_out, ffn_norm, ffn_w_gate, ffn_w_up, ffn_w_down, moe_w_router, moe_b_router, moe_w_gate, moe_w_up, moe_w_down):
    bsz, seq, d_model = x.shape
    depth = w_in.shape[0]
    n = bsz * seq
    nsa_slopes = _alibi_slopes(NSA_HEADS)
    dil_slopes = _alibi_slopes(DIL_HEADS)
    ovt, ktab, ctab = _sel_tables(seq)
    x2 = x.reshape(n, d_model)
    w_main, w_gate = _prep_in_proj(w_in)
    wb_a, wb_b, wb_c, wb_out = (w.astype(BF16) for w in (w_branch_a, w_branch_b, w_branch_c, w_out))
    wf_g, wf_u, wf_d = (w.astype(BF16) for w in (ffn_w_gate, ffn_w_up, ffn_w_down))
    wm_g, wm_u, wm_d = (w.astype(BF16).reshape((-1,) + w.shape[2:]) for w in (moe_w_gate, moe_w_up, moe_w_down))
    cw1, cw2, pw = cmp_w1.astype(BF16), cmp_w2.astype(BF16), pool_w.astype(BF16)
    for l in range(depth):
        z, gates = _in_proj(x2, mix_norm[l][None, :], w_main, w_gate, l)
        z4 = z.reshape(N_SLABS, bsz, seq, LANES)
        y_a = _nsa(z4, gates, nsa_q_norm[l][None, :], nsa_k_norm[l], cw1[l], cmp_b1[l], cw2[l],
                   nsa_slopes, ovt, ktab, ctab)
        y_b = _dil(z4, dil_q_norm[l][None, :], dil_k_norm[l][None, :], dil_slopes)
        y_c = _pool(z4, pw[l], pool_scale[l][None, :])
        merged = _merge(y_a, y_b, y_c, wb_a, wb_b, wb_c, z, l)
        x2 = _out_proj(x2, merged, wb_out, l)
        i = l // 2
        if l % 2 == 0:
            x2 = _ffn(x2, ffn_norm[l][None, :], wf_g, wf_u, wf_d, i)
        else:
            w_r = jnp.zeros((d_model, LANES), F32).at[:, :N_EXPERTS].set(moe_w_router[i])
            b_r = jnp.zeros((1, LANES), F32).at[0, :N_EXPERTS].set(moe_b_router[i])
            x2 = _moe(x2, ffn_norm[l][None, :], w_r, b_r, wm_g, wm_u, wm_d, i)
    return x2.reshape(bsz, seq, d_model)
```

```python
import functools

import numpy as np
import jax
import jax.numpy as jnp
from jax import lax
from jax.experimental import pallas as pl
from jax.experimental.pallas import tpu as pltpu

F32 = jnp.float32
BF16 = jnp.bfloat16

HEAD_DIM = 128
NORM_EPS = 1e-6
NEG_INF = -1e30
NSA_HEADS = 6
NSA_KV_GROUPS = 2
NSA_HPG = NSA_HEADS // NSA_KV_GROUPS
NSA_BRANCHES = 3
CMP_BLOCK = 32
CMP_STRIDE = 16
SEL_BLOCK = 64
SEL_TOP = 8
SEL_FORCE = 1e4
WIN = 512
DIL_PATTERNS = ((128, 1), (512, 4), (2048, 16))
DIL_HPG = 2
DIL_HEADS = DIL_HPG * len(DIL_PATTERNS)
DIL_SPAN = 128
POOL_WINDOWS = (2, 4, 8, 16)
N_EXPERTS = 8
TOP_K = 2
SCALE = HEAD_DIM ** -0.5

LANES = 128
MERGE_SLABS = 3 * 16
Q_SLAB0 = MERGE_SLABS
KV_SLAB0 = Q_SLAB0 + NSA_HEADS
DIL_SLAB0 = KV_SLAB0 + NSA_BRANCHES * 2 * NSA_KV_GROUPS
POOL_SLAB0 = DIL_SLAB0 + 3 * DIL_HEADS
N_SLABS = POOL_SLAB0 + len(POOL_WINDOWS)

VMEM_LIMIT = 56 * 1024 * 1024


def _cparams(sem):
    return pltpu.CompilerParams(dimension_semantics=sem, vmem_limit_bytes=VMEM_LIMIT)


def _rms(x, g):
    return x * lax.rsqrt(jnp.mean(x * x, axis=-1, keepdims=True) + NORM_EPS) * g


def _dot(a, b):
    return jnp.dot(a, b, preferred_element_type=F32)


def _dot_nt(a, b):
    return lax.dot_general(a, b, (((1,), (1,)), ((), ())), preferred_element_type=F32)


def _alibi_slopes(n):
    return jnp.asarray(2.0 ** (-8.0 * np.arange(1, n + 1) / n), dtype=F32)


IN_TN = 1024
IN_SLABS_PER_TILE = IN_TN // LANES
IN_SIG_TILES = MERGE_SLABS // IN_SLABS_PER_TILE


def _in_proj_kernel(x_ref, g_ref, w_ref, wg_ref, z_ref, gate_ref, h_ref):
    j = pl.program_id(1)

    @pl.when(j == 0)
    def _():
        h = _rms(x_ref[...], g_ref[...]).astype(BF16)
        h_ref[...] = h
        for g in range(NSA_KV_GROUPS):
            gate_ref[g] = jax.nn.sigmoid(_dot(h, wg_ref[g]))

    acc = _dot(h_ref[...], w_ref[...])
    is_gate = j < IN_SIG_TILES
    for c in range(IN_SLABS_PER_TILE):
        a = acc[:, c * LANES:(c + 1) * LANES]
        z_ref[c] = jnp.where(is_gate, jax.nn.sigmoid(a), a).astype(BF16)


def _in_proj(x2, g, w_main, w_gate, layer, tm=1024):
    n, d = x2.shape
    n_tiles = w_main.shape[2] // IN_TN
    return pl.pallas_call(
        _in_proj_kernel,
        grid=(n // tm, n_tiles),
        in_specs=[
            pl.BlockSpec((tm, d), lambda i, j: (i, 0)),
            pl.BlockSpec((1, d), lambda i, j: (0, 0)),
            pl.BlockSpec((None, d, IN_TN), lambda i, j: (layer, 0, j)),
            pl.BlockSpec((None, NSA_KV_GROUPS, d, LANES), lambda i, j: (layer, 0, 0, 0)),
        ],
        out_specs=[
            pl.BlockSpec((IN_SLABS_PER_TILE, tm, LANES), lambda i, j: (j, i, 0)),
            pl.BlockSpec((NSA_KV_GROUPS, tm, LANES), lambda i, j: (0, i, 0)),
        ],
        out_shape=[
            jax.ShapeDtypeStruct((N_SLABS, n, LANES), BF16),
            jax.ShapeDtypeStruct((NSA_KV_GROUPS, n, LANES), F32),
        ],
        scratch_shapes=[pltpu.VMEM((tm, d), BF16)],
        compiler_params=_cparams(("parallel", "arbitrary")),
    )(x2, g, w_main, w_gate)


NSA_TQ = 256
LOG2E = 1.4426950408889634
NSA_POS_SPLIT = 64
NSA_SEL_LANE0 = 8
NSA_BIG = 2.0 ** 100


def _key_table(pos, block=None):
    tab = np.zeros((len(pos), LANES), np.float32)
    tab[:, 0:3] = (pos // NSA_POS_SPLIT * NSA_POS_SPLIT)[:, None]
    tab[:, 3:6] = (pos % NSA_POS_SPLIT)[:, None]
    if block is not None:
        tab[np.arange(len(pos)), NSA_SEL_LANE0 + pos // block] = 1.0
    return tab


def _slope_row(slope):
    lane = lax.broadcasted_iota(jnp.int32, (1, LANES), 1)
    sv = jnp.full((1, LANES), slope, F32)
    hi = sv.astype(BF16).astype(F32)
    mid = (sv - hi).astype(BF16).astype(F32)
    lo = sv - hi - mid
    part = jnp.where((lane == 0) | (lane == 3), hi, jnp.where((lane == 1) | (lane == 4), mid, lo))
    return jnp.where(lane < 6, part, 0.0)


def _nsa_kernel(slope_ref, q_ref, kc_ref, vc_ref, ks_ref, vs_ref, kw_ref, vw_ref, gate_ref,
                qg_ref, kg_ref, w1_ref, b1_ref, w2_ref, ovt_ref, ktab_ref, ctab_ref, o_ref,
                tmp32, kca_s, vcn_s, ksa_s, kwa_s, qa_s, osel_s, vsa_s, vwa_s):
    g = pl.program_id(1)
    qi = pl.program_id(2)
    tq = NSA_TQ
    seq = ks_ref.shape[2]
    n_cmp_pad = seq // CMP_STRIDE
    n_sel = seq // SEL_BLOCK
    hd = HEAD_DIM

    @pl.when(qi == 0)
    def _():
        for which, src in enumerate((kc_ref, vc_ref)):
            tmp32[...] = src[0, 0].astype(F32)
            lo = jnp.zeros((n_cmp_pad, HEAD_DIM), F32)
            hi = jnp.zeros((n_cmp_pad, HEAD_DIM), F32)
            for l in range(CMP_STRIDE):
                xl = tmp32[pl.ds(l, n_cmp_pad, stride=CMP_STRIDE), :].astype(BF16)
                lo = lo + _dot(xl, w1_ref[which, l])
                hi = hi + _dot(xl, w1_ref[which, CMP_STRIDE + l])
            hid = jax.nn.gelu(lo + pltpu.roll(hi, n_cmp_pad - 1, axis=0) + b1_ref[which:which + 1, :])
            c = _dot(hid.astype(BF16), w2_ref[which])
            if which == 0:
                kca_s[:, :hd] = _rms(c, kg_ref[0:1, :]).astype(BF16)
                kca_s[:, hd:] = ctab_ref[...]
            else:
                vcn_s[...] = c.astype(BF16)
        ksa_s[:, :hd] = _rms(ks_ref[0, 0].astype(F32), kg_ref[1:2, :]).astype(BF16)
        ksa_s[:, hd:] = ktab_ref[...]
        kwa_s[:, :hd] = _rms(kw_ref[0, 0].astype(F32), kg_ref[2:3, :]).astype(BF16)
        kwa_s[:, hd:] = ktab_ref[...]
        for v_ref, va_s in ((vs_ref, vsa_s), (vw_ref, vwa_s)):
            va_s[:, :hd] = v_ref[0, 0]
            va_s[:, hd:] = jnp.ones((seq, LANES), BF16)

    q0 = qi * tq
    t_col = q0 + lax.broadcasted_iota(jnp.int32, (tq, 1), 0)
    rows_of = lambda hh: slice(hh * tq, (hh + 1) * tq)
    slope_rows = []
    for hh in range(NSA_HPG):
        qs = (_rms(q_ref[hh, 0].astype(F32), qg_ref[...]) * (SCALE * LOG2E)).astype(BF16)
        slope_rows.append(_slope_row(slope_ref[g * NSA_HPG + hh] * LOG2E))
        qa_s[0, rows_of(hh), :hd] = qs
        qa_s[1, rows_of(hh), :hd] = qs
        qa_s[0, rows_of(hh), hd:] = jnp.broadcast_to(slope_rows[hh].astype(BF16), (tq, LANES))

    def stack(x):
        return jnp.concatenate([x] * NSA_HPG, axis=0)

    n_row = lax.broadcasted_iota(jnp.int32, (1, n_cmp_pad), 1)
    c_end = n_row * CMP_STRIDE + (CMP_BLOCK - 1)
    valid_c = (c_end <= t_col) & (n_row < n_cmp_pad - 1)
    s = _dot_nt(qa_s[0], kca_s[...]) + stack(jnp.where(valid_c, 0.0, NEG_INF))
    e = jnp.exp2(s - jnp.max(s, axis=-1, keepdims=True))
    p = e * (stack(valid_c.astype(F32)) / jnp.sum(e, axis=-1, keepdims=True))
    o_cmp = _dot(p.astype(BF16), vcn_s[...])
    psum = functools.reduce(lambda a, b: a + b, [p[rows_of(hh)] for hh in range(NSA_HPG)])

    imp_t = lax.dot_general(ovt_ref[...], psum, (((1,), (1,)), ((), ())),
                            precision=lax.Precision.HIGHEST, preferred_element_type=F32)
    blk = lax.broadcasted_iota(jnp.int32, (n_sel, tq), 0)
    cur = lax.shift_right_logical(q0 + lax.broadcasted_iota(jnp.int32, (n_sel, tq), 1),
                                  int(np.log2(SEL_BLOCK)))
    forced = (blk == 0) | (blk == cur) | (blk == cur - 1)
    imp = jnp.where(forced, SEL_FORCE, jnp.where(blk <= cur, imp_t, -1.0))
    rank = jnp.zeros((n_sel, tq), jnp.int32)
    for i in range(n_sel):
        ri = imp[i:i + 1, :]
        beats = (ri > imp) | ((ri == imp) & (blk > i))
        rank = rank + beats.astype(jnp.int32)
    sel_lanes = jnp.where(rank < min(SEL_TOP, n_sel), 0.0, -NSA_BIG)
    sel_lanes = jnp.concatenate(
        [jnp.zeros((NSA_SEL_LANE0, tq), F32), sel_lanes,
         jnp.zeros((LANES - NSA_SEL_LANE0 - n_sel, tq), F32)], axis=0).T
    for hh in range(NSA_HPG):
        qa_s[1, rows_of(hh), hd:] = (slope_rows[hh] + sel_lanes).astype(BF16)

    def attend(q, k, v, mask):
        s = _dot_nt(q, k) + mask
        e = jnp.exp2(s - jnp.max(s, axis=-1, keepdims=True))
        o = _dot(e.astype(BF16), v)
        return o[:, :hd] / o[:, hd:]

    k_pos = lax.broadcasted_iota(jnp.int32, (1, seq), 1)
    s_mask = jnp.where(k_pos <= t_col, 0.0, NEG_INF)

    w_len = WIN + tq
    w0 = pl.multiple_of(jnp.maximum(q0 - WIN, 0), tq)
    w_dist = t_col - (w0 + lax.broadcasted_iota(jnp.int32, (1, w_len), 1))
    w_mask = jnp.where(w_dist >= 0, jnp.where(w_dist < WIN, 0.0, NEG_INF), NEG_INF)
    kwin = kwa_s[pl.ds(w0, w_len), :]
    vwin = vwa_s[pl.ds(w0, w_len), :]

    nq = seq // tq
    for lo, hi in ((0, nq // 2), (nq // 2, nq)):
        @pl.when((qi >= lo) & (qi < hi))
        def _(nk=hi * tq):
            for hh in range(NSA_HPG):
                osel_s[hh] = attend(qa_s[1, rows_of(hh)], ksa_s[:nk, :], vsa_s[:nk, :], s_mask[:, :nk])

    gates = gate_ref[0]
    for hh in range(NSA_HPG):
        o_win = attend(qa_s[0, rows_of(hh)], kwin, vwin, w_mask)
        out = (gates[:, hh:hh + 1] * o_cmp[rows_of(hh)]
               + gates[:, NSA_HPG + hh:NSA_HPG + hh + 1] * osel_s[hh]
               + gates[:, 2 * NSA_HPG + hh:2 * NSA_HPG + hh + 1] * o_win)
        o_ref[:, hh * HEAD_DIM:(hh + 1) * HEAD_DIM] = out.astype(BF16)


def _nsa(z4, gates, q_gain, k_gain, w1, b1, w2, slopes, ovt, ktab, ctab):
    _, bsz, seq, _ = z4.shape
    tq = NSA_TQ
    nq = seq // tq
    n = bsz * seq

    def kv_spec(br, kvi):
        base = KV_SLAB0 + br * 2 * NSA_KV_GROUPS + kvi * NSA_KV_GROUPS
        return pl.BlockSpec((1, 1, seq, HEAD_DIM), lambda b, g, qi: (base + g, b, 0, 0))

    full = lambda shape: pl.BlockSpec(shape, lambda b, g, qi: (0,) * len(shape))
    return pl.pallas_call(
        _nsa_kernel,
        grid=(bsz, NSA_KV_GROUPS, nq),
        in_specs=[
            pl.BlockSpec(memory_space=pltpu.SMEM),
            pl.BlockSpec((NSA_HPG, 1, tq, HEAD_DIM), lambda b, g, qi: (Q_SLAB0 // NSA_HPG + g, b, qi, 0)),
            kv_spec(0, 0), kv_spec(0, 1), kv_spec(1, 0), kv_spec(1, 1), kv_spec(2, 0), kv_spec(2, 1),
            pl.BlockSpec((1, tq, LANES), lambda b, g, qi: (g, b * nq + qi, 0)),
            full((1, HEAD_DIM)), full((NSA_BRANCHES, HEAD_DIM)),
            full((2, CMP_BLOCK, HEAD_DIM, HEAD_DIM)), full((2, HEAD_DIM)), full((2, HEAD_DIM, HEAD_DIM)),
            full(ovt.shape), full(ktab.shape), full(ctab.shape),
        ],
        out_specs=pl.BlockSpec((tq, NSA_HPG * HEAD_DIM), lambda b, g, qi: (b * nq + qi, g)),
        out_shape=jax.ShapeDtypeStruct((n, NSA_HEADS * HEAD_DIM), BF16),
        scratch_shapes=[
            pltpu.VMEM((seq, HEAD_DIM), F32),
            pltpu.VMEM((seq // CMP_STRIDE, HEAD_DIM + LANES), BF16),
            pltpu.VMEM((seq // CMP_STRIDE, HEAD_DIM), BF16),
            pltpu.VMEM((seq, HEAD_DIM + LANES), BF16),
            pltpu.VMEM((seq, HEAD_DIM + LANES), BF16),
            pltpu.VMEM((2, NSA_HPG * tq, HEAD_DIM + LANES), BF16),
            pltpu.VMEM((NSA_HPG, tq, HEAD_DIM), F32),
            pltpu.VMEM((seq, HEAD_DIM + LANES), BF16),
            pltpu.VMEM((seq, HEAD_DIM + LANES), BF16),
        ],
        compiler_params=_cparams(("parallel", "parallel", "arbitrary")),
    )(slopes, z4, z4, z4, z4, z4, z4, z4, gates, q_gain, k_gain, w1, b1, w2, ovt, ktab, ctab)


DIL_TQ = 256


def _dil_kernel(slope_ref, *refs):
    n_pat = len(DIL_PATTERNS)
    qkv_refs = refs[:3 * n_pat]
    qg_ref, kg_ref, o_ref, qs, ks, vs, num_s, m_s, den_s = refs[3 * n_pat:]
    h = pl.program_id(1)
    seq = qs.shape[0]

    for p, (w, r) in enumerate(DIL_PATTERNS):
        assert w // r == DIL_SPAN
        q_ref, k_ref, v_ref = qkv_refs[3 * p:3 * p + 3]
        qs[...] = _rms(q_ref[0, 0].astype(F32), qg_ref[...])
        ks[...] = _rms(k_ref[0, 0].astype(F32), kg_ref[...])
        vs[...] = v_ref[0, 0].astype(F32)
        slope_r = slope_ref[DIL_HPG * p + h] * float(r)
        length = seq // r
        tq = min(DIL_TQ, length)
        nk = min(length, tq + DIL_SPAN)
        cpt = DIL_TQ // tq
        assert length & (length - 1) == 0 and r % cpt == 0 and (cpt == 1 or nk == length)

        def rows(start, size, r=r):
            return pl.ds(start, size) if r == 1 else pl.ds(start, size, stride=r)

        def tile(idx, carry, p=p, r=r, tq=tq, nk=nk, cpt=cpt, length=length, slope_r=slope_r, rows=rows):
            if cpt == 1:
                c = idx // (length // tq)
                u0 = (idx % (length // tq)) * tq
                k0 = jnp.maximum(u0 - DIL_SPAN, 0)
                rqs, rks = [rows(c + u0 * r, tq)], [rows(c + k0 * r, nk)]
            else:
                u0 = k0 = 0
                rqs = rks = [rows(idx * cpt + j, length) for j in range(cpt)]
            gather = lambda ref, rr: jnp.concatenate([ref[x, :] for x in rr], axis=0).astype(BF16)
            q, k, v = gather(qs, rqs), gather(ks, rks), gather(vs, rks)
            i_row = lax.broadcasted_iota(jnp.int32, (cpt * tq, 1), 0)
            i_col = lax.broadcasted_iota(jnp.int32, (1, cpt * nk), 1)
            du = (u0 + (i_row & (length - 1))) - (k0 + (i_col & (length - 1)))
            s = _dot_nt(q, k) * SCALE - slope_r * du.astype(F32)
            s = jnp.where(du >= 0, jnp.where(du <= DIL_SPAN, s, NEG_INF), NEG_INF)
            if cpt > 1:
                s = jnp.where(((i_row ^ i_col) & -length) == 0, s, NEG_INF)
            m = jnp.max(s, axis=-1, keepdims=True)
            e = jnp.exp(s - m)
            den = jnp.sum(e, axis=-1, keepdims=True)
            num = _dot(e.astype(BF16), v)
            for j, rq in enumerate(rqs):
                sl = slice(j * tq, (j + 1) * tq)
                num_s[p, rq, :] = num[sl]
                m_s[p, rq, :] = jnp.broadcast_to(m[sl], (tq, HEAD_DIM))
                den_s[p, rq, :] = jnp.broadcast_to(den[sl], (tq, HEAD_DIM))
            return carry

        lax.fori_loop(0, r * (length // tq) // cpt, tile, 0)

    m_all = [m_s[p] for p in range(n_pat)]
    m_max = functools.reduce(jnp.maximum, m_all)
    numer = jnp.zeros((seq, HEAD_DIM), F32)
    denom = jnp.zeros((seq, HEAD_DIM), F32)
    for p in range(n_pat):
        wt = jnp.exp(m_all[p] - m_max)
        numer = numer + wt * num_s[p]
        denom = denom + wt * den_s[p]
    o_ref[...] = (numer / denom).astype(BF16)


def _dil(z4, q_gain, k_gain, slopes):
    _, bsz, seq, _ = z4.shape
    n = bsz * seq
    n_pat = len(DIL_PATTERNS)

    def spec(which, p):
        base = DIL_SLAB0 + which * DIL_HEADS + DIL_HPG * p
        return pl.BlockSpec((1, 1, seq, HEAD_DIM), lambda b, h: (base + h, b, 0, 0))

    qkv_specs = [spec(which, p) for p in range(n_pat) for which in range(3)]
    gain = pl.BlockSpec((1, HEAD_DIM), lambda b, h: (0, 0))
    return pl.pallas_call(
        _dil_kernel,
        grid=(bsz, DIL_HPG),
        in_specs=[pl.BlockSpec(memory_space=pltpu.SMEM)] + qkv_specs + [gain, gain],
        out_specs=pl.BlockSpec((seq, HEAD_DIM), lambda b, h: (b, h)),
        out_shape=jax.ShapeDtypeStruct((n, DIL_HPG * HEAD_DIM), BF16),
        scratch_shapes=[pltpu.VMEM((seq, HEAD_DIM), F32)] * 3
        + [pltpu.VMEM((n_pat, seq, HEAD_DIM), F32)] * 3,
        compiler_params=_cparams(("parallel", "parallel")),
    )(slopes, *([z4] * (3 * n_pat)), q_gain, k_gain)


def _pool_kernel(u_ref, w_ref, sc_ref, o_ref):
    seq = u_ref.shape[2]
    t = lax.broadcasted_iota(jnp.int32, (seq, 1), 0)
    for g, w in enumerate(POOL_WINDOWS):
        u = u_ref[g, 0].astype(F32)
        acc = u
        span = 1
        while span < w:
            acc = acc + jnp.where(t >= span, pltpu.roll(acc, span, axis=0), 0.0)
            span *= 2
        d = acc / jnp.minimum(t + 1, w).astype(F32) - u
        y = _dot(d.astype(BF16), w_ref[g]) * sc_ref[:, g * LANES:(g + 1) * LANES]
        o_ref[:, g * LANES:(g + 1) * LANES] = y.astype(BF16)


def _pool(z4, w_pool, scale):
    _, bsz, seq, _ = z4.shape
    ng = len(POOL_WINDOWS)
    assert all(w & (w - 1) == 0 for w in POOL_WINDOWS) and POOL_SLAB0 % ng == 0
    return pl.pallas_call(
        _pool_kernel,
        grid=(bsz,),
        in_specs=[
            pl.BlockSpec((ng, 1, seq, LANES), lambda b: (POOL_SLAB0 // ng, b, 0, 0)),
            pl.BlockSpec((ng, LANES, LANES), lambda b: (0, 0, 0)),
            pl.BlockSpec((1, ng * LANES), lambda b: (0, 0)),
        ],
        out_specs=pl.BlockSpec((seq, ng * LANES), lambda b: (b, 0)),
        out_shape=jax.ShapeDtypeStruct((bsz * seq, ng * LANES), BF16),
        compiler_params=_cparams(("parallel",)),
    )(z4, w_pool, scale)


MERGE_TN = 512


def _merge_out_kernel(x_ref, ya_ref, yb_ref, yc_ref, g_ref, wa_ref, wb_ref, wc_ref, wo_ref, o_ref, m_ref):
    d = o_ref.shape[1]
    n_gate = d // LANES
    ya, yb, yc = ya_ref[...], yb_ref[...], yc_ref[...]
    for j in range(d // MERGE_TN):
        cols = slice(j * MERGE_TN, (j + 1) * MERGE_TN)
        a = _dot(ya, wa_ref[:, cols])
        b = _dot(yb, wb_ref[:, cols])
        c = _dot(yc, wc_ref[:, cols])
        for s in range(MERGE_TN // LANES):
            sl = slice(s * LANES, (s + 1) * LANES)
            slab = j * (MERGE_TN // LANES) + s
            m_ref[:, slab * LANES:(slab + 1) * LANES] = (
                g_ref[slab].astype(F32) * a[:, sl] + g_ref[n_gate + slab].astype(F32) * b[:, sl]
                + g_ref[2 * n_gate + slab].astype(F32) * c[:, sl]).astype(BF16)
    o_ref[...] = x_ref[...] + _dot(m_ref[...], wo_ref[...])


def _merge_out(x2, ya, yb, yc, wa, wb, wc, w_out, z, layer, tm=512):
    n, d = x2.shape
    act = lambda a: pl.BlockSpec((tm, a.shape[1]), lambda i: (i, 0))
    wgt = lambda w: pl.BlockSpec((None,) + w.shape[1:], lambda i: (layer, 0, 0), pipeline_mode=pl.Buffered(1))
    return pl.pallas_call(
        _merge_out_kernel,
        grid=(n // tm,),
        in_specs=[act(x2), act(ya), act(yb), act(yc),
                  pl.BlockSpec((MERGE_SLABS, tm, LANES), lambda i: (0, i, 0)),
                  wgt(wa), wgt(wb), wgt(wc), wgt(w_out)],
        out_specs=pl.BlockSpec((tm, d), lambda i: (i, 0)),
        out_shape=jax.ShapeDtypeStruct((n, d), F32),
        scratch_shapes=[pltpu.VMEM((tm, d), BF16)],
        compiler_params=_cparams(("parallel",)),
    )(x2, ya, yb, yc, z, wa, wb, wc, w_out)


FFN_TF = 512


def _ffn_kernel(x_ref, g_ref, wg_ref, wu_ref, wd_ref, o_ref, h_ref):
    f = pl.program_id(1)

    @pl.when(f == 0)
    def _():
        x = x_ref[...]
        h_ref[...] = _rms(x, g_ref[...]).astype(BF16)
        o_ref[...] = x

    h = h_ref[...]
    a = (jax.nn.silu(_dot(h, wg_ref[...])) * _dot(h, wu_ref[...])).astype(BF16)
    o_ref[...] += _dot(a, wd_ref[...])


def _ffn(x2, g, w_gate, w_up, w_down, layer, tm=512):
    n, d = x2.shape
    d_ff = w_gate.shape[2]
    return pl.pallas_call(
        _ffn_kernel,
        grid=(n // tm, d_ff // FFN_TF),
        in_specs=[
            pl.BlockSpec((tm, d), lambda i, f: (i, 0)),
            pl.BlockSpec((1, d), lambda i, f: (0, 0)),
            pl.BlockSpec((None, d, FFN_TF), lambda i, f: (layer, 0, f)),
            pl.BlockSpec((None, d, FFN_TF), lambda i, f: (layer, 0, f)),
            pl.BlockSpec((None, FFN_TF, d), lambda i, f: (layer, f, 0)),
        ],
        out_specs=pl.BlockSpec((tm, d), lambda i, f: (i, 0)),
        out_shape=jax.ShapeDtypeStruct((n, d), F32),
        scratch_shapes=[pltpu.VMEM((tm, d), BF16)],
        compiler_params=_cparams(("parallel", "arbitrary")),
    )(x2, g, w_gate, w_up, w_down)


def _router_kernel(x_ref, g_ref, w_ref, b_ref, idx_ref, wt_ref):
    h = _rms(x_ref[...], g_ref[...])
    logits = jnp.dot(h, w_ref[...], precision=lax.Precision.HIGHEST, preferred_element_type=F32) + b_ref[...]
    lane = lax.broadcasted_iota(jnp.int32, logits.shape, 1)
    l1 = jnp.where(lane < N_EXPERTS, logits, -jnp.inf)
    m1 = jnp.max(l1, axis=-1, keepdims=True)
    i1 = jnp.min(jnp.where(l1 == m1, lane, LANES), axis=-1, keepdims=True)
    l2 = jnp.where(lane == i1, -jnp.inf, l1)
    m2 = jnp.max(l2, axis=-1, keepdims=True)
    i2 = jnp.min(jnp.where(l2 == m2, lane, LANES), axis=-1, keepdims=True)
    e = jnp.exp(m2 - m1)
    w1 = 1.0 / (1.0 + e)
    idx_ref[...] = jnp.where(lane == 0, i1, jnp.where(lane == 1, i2, 0))
    wt_ref[...] = jnp.where(lane == 0, w1, jnp.where(lane == 1, e * w1, 0.0))


def _router(x2, g, w_r, b_r, tm=512):
    n, d = x2.shape
    return pl.pallas_call(
        _router_kernel,
        grid=(n // tm,),
        in_specs=[
            pl.BlockSpec((tm, d), lambda i: (i, 0)),
            pl.BlockSpec((1, d), lambda i: (0, 0)),
            pl.BlockSpec((d, LANES), lambda i: (0, 0)),
            pl.BlockSpec((1, LANES), lambda i: (0, 0)),
        ],
        out_specs=[pl.BlockSpec((tm, LANES), lambda i: (i, 0))] * 2,
        out_shape=[jax.ShapeDtypeStruct((n, LANES), jnp.int32), jax.ShapeDtypeStruct((n, LANES), F32)],
        compiler_params=_cparams(("parallel",)),
    )(x2, g, w_r, b_r)


MOE_TM = 512
MOE_TF = 512


def _moe_kernel(tile_e_ref, n_used_ref, row_tok_ref, x_hbm, g_ref, wg_ref, wu_ref, wd_ref, o_ref,
                xbuf, h_ref, sem):
    t = pl.program_id(0)
    f = pl.program_id(1)
    tm = MOE_TM
    n_used = n_used_ref[0]
    used = t < n_used
    slot = lax.rem(t, 2)

    def row_copy(tile, r, s):
        tok = row_tok_ref[tile * tm + r]
        return pltpu.make_async_copy(x_hbm.at[pl.ds(tok, 1), :], xbuf.at[s, pl.ds(r, 1), :], sem.at[s])

    def start_tile(tile, s):
        lax.fori_loop(0, tm, lambda r, c: (row_copy(tile, r, s).start(), c)[1], 0, unroll=8)

    def wait_tile(tile, s):
        lax.fori_loop(0, tm, lambda r, c: (row_copy(tile, r, s).wait(), c)[1], 0, unroll=8)

    @pl.when(f == 0)
    def _():
        o_ref[...] = jnp.zeros_like(o_ref)

    @pl.when((f == 0) & (t == 0))
    def _():
        start_tile(0, 0)

    @pl.when((f == 0) & used)
    def _():
        wait_tile(t, slot)

        @pl.when(t + 1 < n_used)
        def _():
            start_tile(t + 1, 1 - slot)

        h_ref[...] = _rms(xbuf[slot], g_ref[...]).astype(BF16)

    @pl.when(used)
    def _():
        h = h_ref[...]
        a = (jax.nn.silu(_dot(h, wg_ref[...])) * _dot(h, wu_ref[...])).astype(BF16)
        o_ref[...] += _dot(a, wd_ref[...])


def _moe_experts(x2, g, w_gate, w_up, w_down, layer, tile_e, n_used, row_tok):
    n, d = x2.shape
    d_ff = w_gate.shape[2]
    n_f = d_ff // MOE_TF
    n_tiles = tile_e.shape[0]
    e0 = layer * N_EXPERTS

    def f_eff(t, f, nu):
        return jnp.where(t < nu[0], f, n_f - 1)

    grid_spec = pltpu.PrefetchScalarGridSpec(
        num_scalar_prefetch=3,
        grid=(n_tiles, n_f),
        in_specs=[
            pl.BlockSpec(memory_space=pl.ANY),
            pl.BlockSpec((1, d), lambda t, f, te, nu, rt: (0, 0)),
            pl.BlockSpec((None, d, MOE_TF), lambda t, f, te, nu, rt: (e0 + te[t], 0, f_eff(t, f, nu))),
            pl.BlockSpec((None, d, MOE_TF), lambda t, f, te, nu, rt: (e0 + te[t], 0, f_eff(t, f, nu))),
            pl.BlockSpec((None, MOE_TF, d), lambda t, f, te, nu, rt: (e0 + te[t], f_eff(t, f, nu), 0)),
        ],
        out_specs=pl.BlockSpec((MOE_TM, d), lambda t, f, te, nu, rt: (t, 0)),
        scratch_shapes=[
            pltpu.VMEM((2, MOE_TM, d), F32),
            pltpu.VMEM((MOE_TM, d), BF16),
            pltpu.SemaphoreType.DMA((2,)),
        ],
    )
    return pl.pallas_call(
        _moe_kernel,
        grid_spec=grid_spec,
        out_shape=jax.ShapeDtypeStruct((n_tiles * MOE_TM, d), F32),
        compiler_params=_cparams(("arbitrary", "arbitrary")),
    )(tile_e, n_used, row_tok, x2, g, w_gate, w_up, w_down)


COMBINE_TM = 256


def _combine_kernel(pos_ref, x_ref, wt_ref, y_hbm, o_ref, buf, sem):
    i = pl.program_id(0)
    tm = COMBINE_TM
    slot = lax.rem(i, 2)

    def row_copy(tile, r, k, s):
        src = pos_ref[(tile * tm + r) * TOP_K + k]
        return pltpu.make_async_copy(y_hbm.at[pl.ds(src, 1), :], buf.at[s, k, pl.ds(r, 1), :], sem.at[s])

    def start_tile(tile, s):
        def body(r, c):
            for k in range(TOP_K):
                row_copy(tile, r, k, s).start()
            return c
        lax.fori_loop(0, tm, body, 0, unroll=8)

    def wait_tile(tile, s):
        def body(r, c):
            for k in range(TOP_K):
                row_copy(tile, r, k, s).wait()
            return c
        lax.fori_loop(0, tm, body, 0, unroll=8)

    @pl.when(i == 0)
    def _():
        start_tile(0, 0)

    wait_tile(i, slot)

    @pl.when(i + 1 < pl.num_programs(0))
    def _():
        start_tile(i + 1, 1 - slot)

    wt = wt_ref[...]
    acc = x_ref[...]
    for k in range(TOP_K):
        acc = acc + wt[:, k:k + 1] * buf[slot, k]
    o_ref[...] = acc


def _combine(x2, wts, ys, pos):
    n, d = x2.shape
    grid_spec = pltpu.PrefetchScalarGridSpec(
        num_scalar_prefetch=1,
        grid=(n // COMBINE_TM,),
        in_specs=[
            pl.BlockSpec((COMBINE_TM, d), lambda i, p: (i, 0)),
            pl.BlockSpec((COMBINE_TM, LANES), lambda i, p: (i, 0)),
            pl.BlockSpec(memory_space=pl.ANY),
        ],
        out_specs=pl.BlockSpec((COMBINE_TM, d), lambda i, p: (i, 0)),
        scratch_shapes=[pltpu.VMEM((2, TOP_K, COMBINE_TM, d), F32), pltpu.SemaphoreType.DMA((2,))],
    )
    return pl.pallas_call(
        _combine_kernel,
        grid_spec=grid_spec,
        out_shape=jax.ShapeDtypeStruct((n, d), F32),
        compiler_params=_cparams(("arbitrary",)),
    )(pos, x2, wts, ys)


def _route_plan(idx, n):
    flat_e = idx[:, :TOP_K].reshape(-1)
    onehot = (flat_e[:, None] == jnp.arange(N_EXPERTS, dtype=jnp.int32)[None, :]).astype(jnp.int32)
    before = jnp.cumsum(onehot, axis=0) - onehot
    rank = jnp.sum(before * onehot, axis=1)
    counts = jnp.sum(onehot, axis=0)
    padded = ((counts + MOE_TM - 1) // MOE_TM) * MOE_TM
    ends = jnp.cumsum(padded)
    pos = (ends - padded)[flat_e] + rank
    n_rows = TOP_K * n + N_EXPERTS * MOE_TM
    row_tok = jnp.zeros((n_rows,), jnp.int32).at[pos].set(jnp.arange(TOP_K * n, dtype=jnp.int32) // TOP_K)
    tile_start = jnp.arange(n_rows // MOE_TM, dtype=jnp.int32) * MOE_TM
    tile_e = jnp.sum((tile_start[:, None] >= ends[None, :]).astype(jnp.int32), axis=1)
    tile_e = jnp.minimum(tile_e, N_EXPERTS - 1)
    n_used = (ends[-1] // MOE_TM).astype(jnp.int32).reshape(1)
    return tile_e, n_used, row_tok, pos.astype(jnp.int32)


def _moe(x2, g, w_r, b_r, w_gate, w_up, w_down, layer):
    n = x2.shape[0]
    idx, wts = _router(x2, g, w_r, b_r)
    tile_e, n_used, row_tok, pos = _route_plan(idx, n)
    ys = _moe_experts(x2, g, w_gate, w_up, w_down, layer, tile_e, n_used, row_tok)
    return _combine(x2, wts, ys, pos)


def _prep_in_proj(w_in):
    d_model = w_in.shape[1]
    q_w = NSA_HEADS * HEAD_DIM
    kv_w = NSA_BRANCHES * 2 * NSA_KV_GROUPS * HEAD_DIM
    gate_w = NSA_BRANCHES * NSA_HEADS
    dil_w = 3 * DIL_HEADS * HEAD_DIM
    pool_w = len(POOL_WINDOWS) * LANES
    o_q, o_kv, o_g = 0, q_w, q_w + kv_w
    o_dil = o_g + gate_w
    o_pool = o_dil + dil_w
    o_merge = o_pool + pool_w
    assert w_in.shape[2] == o_merge + 3 * d_model
    w_bf = w_in.astype(BF16)
    w_main = jnp.concatenate(
        [w_bf[:, :, o_merge:], w_bf[:, :, o_q:o_g], w_bf[:, :, o_dil:o_merge]], axis=2)
    assert w_main.shape[2] == N_SLABS * LANES
    groups = []
    for g in range(NSA_KV_GROUPS):
        parts = []
        for br in range(NSA_BRANCHES):
            c0 = o_g + br * NSA_HEADS + g * NSA_HPG
            parts.append(w_bf[:, :, c0:c0 + NSA_HPG])
        parts.append(jnp.zeros(w_bf.shape[:2] + (LANES - NSA_BRANCHES * NSA_HPG,), BF16))
        groups.append(jnp.concatenate(parts, axis=2))
    w_gate = jnp.stack(groups, axis=1)
    return w_main, w_gate


def _sel_tables(seq):
    n_cmp = seq // CMP_STRIDE - 1
    n_sel = seq // SEL_BLOCK
    i = np.arange(n_cmp)[:, None] * CMP_STRIDE
    j = np.arange(n_sel)[None, :] * SEL_BLOCK
    ov = np.clip(np.minimum(i + CMP_BLOCK, j + SEL_BLOCK) - np.maximum(i, j), 0, None) / CMP_BLOCK
    ovt = np.zeros((n_sel, n_cmp + 1), np.float32)
    ovt[:, :n_cmp] = ov.T
    assert NSA_SEL_LANE0 + n_sel <= LANES
    ktab = _key_table(np.arange(seq), SEL_BLOCK)
    ctab = _key_table(np.arange(n_cmp + 1) * CMP_STRIDE + CMP_BLOCK - 1)
    return jnp.asarray(ovt), jnp.asarray(ktab, dtype=BF16), jnp.asarray(ctab, dtype=BF16)


def kernel(x, mix_norm, w_in, nsa_q_norm, nsa_k_norm, cmp_w1, cmp_b1, cmp_w2, dil_q_norm, dil_k_norm, pool_w, pool_scale, w_branch_a, w_branch_b, w_branch_c, w_out, ffn_norm, ffn_w_gate, ffn_w_up, ffn_w_down, moe_w_router, moe_b_router, moe_w_gate, moe_w_up, moe_w_down):
    bsz, seq, d_model = x.shape
    depth = w_in.shape[0]
    n = bsz * seq
    nsa_slopes = _alibi_slopes(NSA_HEADS)
    dil_slopes = _alibi_slopes(DIL_HEADS)
    ovt, ktab, ctab = _sel_tables(seq)
    x2 = x.reshape(n, d_model)
    w_main, w_gate = _prep_in_proj(w_in)
    wb_a, wb_b, wb_c, wb_out = (w.astype(BF16) for w in (w_branch_a, w_branch_b, w_branch_c, w_out))
    wf_g, wf_u, wf_d = (w.astype(BF16) for w in (ffn_w_gate, ffn_w_up, ffn_w_down))
    wm_g, wm_u, wm_d = (w.astype(BF16).reshape((-1,) + w.shape[2:]) for w in (moe_w_gate, moe_w_up, moe_w_down))
    cw1, cw2, pw = cmp_w1.astype(BF16), cmp_w2.astype(BF16), pool_w.astype(BF16)
    for l in range(depth):
        z, gates = _in_proj(x2, mix_norm[l][None, :], w_main, w_gate, l)
        z4 = z.reshape(N_SLABS, bsz, seq, LANES)
        y_a = _nsa(z4, gates, nsa_q_norm[l][None, :], nsa_k_norm[l], cw1[l], cmp_b1[l], cw2[l],
                   nsa_slopes, ovt, ktab, ctab)
        y_b = _dil(z4, dil_q_norm[l][None, :], dil_k_norm[l][None, :], dil_slopes)
        y_c = _pool(z4, pw[l], pool_scale[l][None, :])
        x2 = _merge_out(x2, y_a, y_b, y_c, wb_a, wb_b, wb_c, wb_out, z, l)
        i = l // 2
        if l % 2 == 0:
            x2 = _ffn(x2, ffn_norm[l][None, :], wf_g, wf_u, wf_d, i)
        else:
            w_r = jnp.zeros((d_model, LANES), F32).at[:, :N_EXPERTS].set(moe_w_router[i])
            b_r = jnp.zeros((1, LANES), F32).at[0, :N_EXPERTS].set(moe_b_router[i])
            x2 = _moe(x2, ffn_norm[l][None, :], w_r, b_r, wm_g, wm_u, wm_d, i)
    return x2.reshape(bsz, seq, d_model)
```

```python
import functools

import numpy as np
import jax
import jax.numpy as jnp
from jax import lax
from jax.experimental import pallas as pl
from jax.experimental.pallas import tpu as pltpu

F32 = jnp.float32
BF16 = jnp.bfloat16

HEAD_DIM = 128
NORM_EPS = 1e-6
NEG_INF = -1e30
NSA_HEADS = 6
NSA_KV_GROUPS = 2
NSA_HPG = NSA_HEADS // NSA_KV_GROUPS
NSA_BRANCHES = 3
CMP_BLOCK = 32
CMP_STRIDE = 16
SEL_BLOCK = 64
SEL_TOP = 8
SEL_FORCE = 1e4
WIN = 512
DIL_PATTERNS = ((128, 1), (512, 4), (2048, 16))
DIL_HPG = 2
DIL_HEADS = DIL_HPG * len(DIL_PATTERNS)
DIL_SPAN = 128
POOL_WINDOWS = (2, 4, 8, 16)
N_EXPERTS = 8
TOP_K = 2
SCALE = HEAD_DIM ** -0.5

LANES = 128
MERGE_SLABS = 3 * 16
Q_SLAB0 = MERGE_SLABS
KV_SLAB0 = Q_SLAB0 + NSA_HEADS
DIL_SLAB0 = KV_SLAB0 + NSA_BRANCHES * 2 * NSA_KV_GROUPS
POOL_SLAB0 = DIL_SLAB0 + 3 * DIL_HEADS
N_SLABS = POOL_SLAB0 + len(POOL_WINDOWS)

VMEM_LIMIT = 56 * 1024 * 1024


def _cparams(sem):
    return pltpu.CompilerParams(dimension_semantics=sem, vmem_limit_bytes=VMEM_LIMIT)


def _rms(x, g):
    return x * lax.rsqrt(jnp.mean(x * x, axis=-1, keepdims=True) + NORM_EPS) * g


def _dot(a, b):
    return jnp.dot(a, b, preferred_element_type=F32)


def _dot_nt(a, b):
    return lax.dot_general(a, b, (((1,), (1,)), ((), ())), preferred_element_type=F32)


def _alibi_slopes(n):
    return jnp.asarray(2.0 ** (-8.0 * np.arange(1, n + 1) / n), dtype=F32)


IN_TN = 1024
IN_SLABS_PER_TILE = IN_TN // LANES
IN_SIG_TILES = MERGE_SLABS // IN_SLABS_PER_TILE


def _in_proj_kernel(x_ref, g_ref, w_ref, wg_ref, z_ref, gate_ref, h_ref):
    j = pl.program_id(1)

    @pl.when(j == 0)
    def _():
        h = _rms(x_ref[...], g_ref[...]).astype(BF16)
        h_ref[...] = h
        for g in range(NSA_KV_GROUPS):
            gate_ref[g] = jax.nn.sigmoid(_dot(h, wg_ref[g]))

    acc = _dot(h_ref[...], w_ref[...])
    is_gate = j < IN_SIG_TILES
    for c in range(IN_SLABS_PER_TILE):
        a = acc[:, c * LANES:(c + 1) * LANES]
        z_ref[c] = jnp.where(is_gate, jax.nn.sigmoid(a), a).astype(BF16)


def _in_proj(x2, g, w_main, w_gate, layer, tm=1024):
    n, d = x2.shape
    n_tiles = w_main.shape[2] // IN_TN
    return pl.pallas_call(
        _in_proj_kernel,
        grid=(n // tm, n_tiles),
        in_specs=[
            pl.BlockSpec((tm, d), lambda i, j: (i, 0)),
            pl.BlockSpec((1, d), lambda i, j: (0, 0)),
            pl.BlockSpec((None, d, IN_TN), lambda i, j: (layer, 0, j)),
            pl.BlockSpec((None, NSA_KV_GROUPS, d, LANES), lambda i, j: (layer, 0, 0, 0)),
        ],
        out_specs=[
            pl.BlockSpec((IN_SLABS_PER_TILE, tm, LANES), lambda i, j: (j, i, 0)),
            pl.BlockSpec((NSA_KV_GROUPS, tm, LANES), lambda i, j: (0, i, 0)),
        ],
        out_shape=[
            jax.ShapeDtypeStruct((N_SLABS, n, LANES), BF16),
            jax.ShapeDtypeStruct((NSA_KV_GROUPS, n, LANES), F32),
        ],
        scratch_shapes=[pltpu.VMEM((tm, d), BF16)],
        compiler_params=_cparams(("parallel", "arbitrary")),
    )(x2, g, w_main, w_gate)


NSA_TQ = 256
LOG2E = 1.4426950408889634
NSA_POS_SPLIT = 64
NSA_SEL_LANE0 = 8
NSA_BIG = 2.0 ** 100


def _key_table(pos, block=None):
    tab = np.zeros((len(pos), LANES), np.float32)
    tab[:, 0:3] = (pos // NSA_POS_SPLIT * NSA_POS_SPLIT)[:, None]
    tab[:, 3:6] = (pos % NSA_POS_SPLIT)[:, None]
    if block is not None:
        tab[np.arange(len(pos)), NSA_SEL_LANE0 + pos // block] = 1.0
    return tab


def _slope_row(slope):
    lane = lax.broadcasted_iota(jnp.int32, (1, LANES), 1)
    sv = jnp.full((1, LANES), slope, F32)
    hi = sv.astype(BF16).astype(F32)
    mid = (sv - hi).astype(BF16).astype(F32)
    lo = sv - hi - mid
    part = jnp.where((lane == 0) | (lane == 3), hi, jnp.where((lane == 1) | (lane == 4), mid, lo))
    return jnp.where(lane < 6, part, 0.0)


def _nsa_kernel(slope_ref, q_ref, kc_ref, vc_ref, ks_ref, vs_ref, kw_ref, vw_ref, gate_ref,
                qg_ref, kg_ref, w1_ref, b1_ref, w2_ref, ovt_ref, ktab_ref, ctab_ref, o_ref,
                tmp32, kca_s, vcn_s, ksa_s, kwa_s, qa_s, osel_s, vsa_s, vwa_s):
    g = pl.program_id(1)
    qi = pl.program_id(2)
    tq = NSA_TQ
    seq = ks_ref.shape[2]
    n_cmp_pad = seq // CMP_STRIDE
    n_sel = seq // SEL_BLOCK
    hd = HEAD_DIM

    @pl.when(qi == 0)
    def _():
        for which, src in enumerate((kc_ref, vc_ref)):
            tmp32[...] = src[0, 0].astype(F32)
            lo = jnp.zeros((n_cmp_pad, HEAD_DIM), F32)
            hi = jnp.zeros((n_cmp_pad, HEAD_DIM), F32)
            for l in range(CMP_STRIDE):
                xl = tmp32[pl.ds(l, n_cmp_pad, stride=CMP_STRIDE), :].astype(BF16)
                lo = lo + _dot(xl, w1_ref[which, l])
                hi = hi + _dot(xl, w1_ref[which, CMP_STRIDE + l])
            hid = jax.nn.gelu(lo + pltpu.roll(hi, n_cmp_pad - 1, axis=0) + b1_ref[which:which + 1, :])
            c = _dot(hid.astype(BF16), w2_ref[which])
            if which == 0:
                kca_s[:, :hd] = _rms(c, kg_ref[0:1, :]).astype(BF16)
                kca_s[:, hd:] = ctab_ref[...]
            else:
                vcn_s[...] = c.astype(BF16)
        ksa_s[:, :hd] = _rms(ks_ref[0, 0].astype(F32), kg_ref[1:2, :]).astype(BF16)
        ksa_s[:, hd:] = ktab_ref[...]
        kwa_s[:, :hd] = _rms(kw_ref[0, 0].astype(F32), kg_ref[2:3, :]).astype(BF16)
        kwa_s[:, hd:] = ktab_ref[...]
        for v_ref, va_s in ((vs_ref, vsa_s), (vw_ref, vwa_s)):
            va_s[:, :hd] = v_ref[0, 0]
            va_s[:, hd:] = jnp.ones((seq, LANES), BF16)

    q0 = qi * tq
    t_col = q0 + lax.broadcasted_iota(jnp.int32, (tq, 1), 0)
    rows_of = lambda hh: slice(hh * tq, (hh + 1) * tq)
    slope_rows = []
    for hh in range(NSA_HPG):
        qs = (_rms(q_ref[hh, 0].astype(F32), qg_ref[...]) * (SCALE * LOG2E)).astype(BF16)
        slope_rows.append(_slope_row(slope_ref[g * NSA_HPG + hh] * LOG2E))
        qa_s[0, rows_of(hh), :hd] = qs
        qa_s[1, rows_of(hh), :hd] = qs
        qa_s[0, rows_of(hh), hd:] = jnp.broadcast_to(slope_rows[hh].astype(BF16), (tq, LANES))

    def stack(x):
        return jnp.concatenate([x] * NSA_HPG, axis=0)

    n_row = lax.broadcasted_iota(jnp.int32, (1, n_cmp_pad), 1)
    c_end = n_row * CMP_STRIDE + (CMP_BLOCK - 1)
    valid_c = (c_end <= t_col) & (n_row < n_cmp_pad - 1)
    s = _dot_nt(qa_s[0], kca_s[...]) + stack(jnp.where(valid_c, 0.0, NEG_INF))
    e = jnp.exp2(s - jnp.max(s, axis=-1, keepdims=True))
    p = e * (stack(valid_c.astype(F32)) / jnp.sum(e, axis=-1, keepdims=True))
    o_cmp = _dot(p.astype(BF16), vcn_s[...])
    psum = functools.reduce(lambda a, b: a + b, [p[rows_of(hh)] for hh in range(NSA_HPG)])

    imp_t = lax.dot_general(ovt_ref[...], psum, (((1,), (1,)), ((), ())),
                            precision=lax.Precision.HIGHEST, preferred_element_type=F32)
    blk = lax.broadcasted_iota(jnp.int32, (n_sel, tq), 0)
    cur = lax.shift_right_logical(q0 + lax.broadcasted_iota(jnp.int32, (n_sel, tq), 1),
                                  int(np.log2(SEL_BLOCK)))
    forced = (blk == 0) | (blk == cur) | (blk == cur - 1)
    imp = jnp.where(forced, SEL_FORCE, jnp.where(blk <= cur, imp_t, -1.0))
    rank = jnp.zeros((n_sel, tq), jnp.int32)
    for i in range(n_sel):
        ri = imp[i:i + 1, :]
        beats = (ri > imp) | ((ri == imp) & (blk > i))
        rank = rank + beats.astype(jnp.int32)
    sel_lanes = jnp.where(rank < min(SEL_TOP, n_sel), 0.0, -NSA_BIG)
    sel_lanes = jnp.concatenate(
        [jnp.zeros((NSA_SEL_LANE0, tq), F32), sel_lanes,
         jnp.zeros((LANES - NSA_SEL_LANE0 - n_sel, tq), F32)], axis=0).T
    for hh in range(NSA_HPG):
        qa_s[1, rows_of(hh), hd:] = (slope_rows[hh] + sel_lanes).astype(BF16)

    def attend(q, k, v, mask):
        s = _dot_nt(q, k) + mask
        e = jnp.exp2(s - jnp.max(s, axis=-1, keepdims=True))
        o = _dot(e.astype(BF16), v)
        return o[:, :hd] / o[:, hd:]

    k_pos = lax.broadcasted_iota(jnp.int32, (1, seq), 1)
    s_mask = jnp.where(k_pos <= t_col, 0.0, NEG_INF)

    w_len = WIN + tq
    w0 = pl.multiple_of(jnp.maximum(q0 - WIN, 0), tq)
    w_dist = t_col - (w0 + lax.broadcasted_iota(jnp.int32, (1, w_len), 1))
    w_mask = jnp.where(w_dist >= 0, jnp.where(w_dist < WIN, 0.0, NEG_INF), NEG_INF)
    kwin = kwa_s[pl.ds(w0, w_len), :]
    vwin = vwa_s[pl.ds(w0, w_len), :]

    nq = seq // tq
    for lo, hi in ((i, i + 2) for i in range(0, nq, 2)):
        @pl.when((qi >= lo) & (qi < hi))
        def _(nk=hi * tq):
            for hh in range(NSA_HPG):
                osel_s[hh] = attend(qa_s[1, rows_of(hh)], ksa_s[:nk, :], vsa_s[:nk, :], s_mask[:, :nk])

    gates = gate_ref[0]
    for hh in range(NSA_HPG):
        o_win = attend(qa_s[0, rows_of(hh)], kwin, vwin, w_mask)
        out = (gates[:, hh:hh + 1] * o_cmp[rows_of(hh)]
               + gates[:, NSA_HPG + hh:NSA_HPG + hh + 1] * osel_s[hh]
               + gates[:, 2 * NSA_HPG + hh:2 * NSA_HPG + hh + 1] * o_win)
        o_ref[:, hh * HEAD_DIM:(hh + 1) * HEAD_DIM] = out.astype(BF16)


def _nsa(z4, gates, q_gain, k_gain, w1, b1, w2, slopes, ovt, ktab, ctab):
    _, bsz, seq, _ = z4.shape
    tq = NSA_TQ
    nq = seq // tq
    n = bsz * seq

    def kv_spec(br, kvi):
        base = KV_SLAB0 + br * 2 * NSA_KV_GROUPS + kvi * NSA_KV_GROUPS
        return pl.BlockSpec((1, 1, seq, HEAD_DIM), lambda b, g, qi: (base + g, b, 0, 0))

    full = lambda shape: pl.BlockSpec(shape, lambda b, g, qi: (0,) * len(shape))
    return pl.pallas_call(
        _nsa_kernel,
        grid=(bsz, NSA_KV_GROUPS, nq),
        in_specs=[
            pl.BlockSpec(memory_space=pltpu.SMEM),
            pl.BlockSpec((NSA_HPG, 1, tq, HEAD_DIM), lambda b, g, qi: (Q_SLAB0 // NSA_HPG + g, b, qi, 0)),
            kv_spec(0, 0), kv_spec(0, 1), kv_spec(1, 0), kv_spec(1, 1), kv_spec(2, 0), kv_spec(2, 1),
            pl.BlockSpec((1, tq, LANES), lambda b, g, qi: (g, b * nq + qi, 0)),
            full((1, HEAD_DIM)), full((NSA_BRANCHES, HEAD_DIM)),
            full((2, CMP_BLOCK, HEAD_DIM, HEAD_DIM)), full((2, HEAD_DIM)), full((2, HEAD_DIM, HEAD_DIM)),
            full(ovt.shape), full(ktab.shape), full(ctab.shape),
        ],
        out_specs=pl.BlockSpec((tq, NSA_HPG * HEAD_DIM), lambda b, g, qi: (b * nq + qi, g)),
        out_shape=jax.ShapeDtypeStruct((n, NSA_HEADS * HEAD_DIM), BF16),
        scratch_shapes=[
            pltpu.VMEM((seq, HEAD_DIM), F32),
            pltpu.VMEM((seq // CMP_STRIDE, HEAD_DIM + LANES), BF16),
            pltpu.VMEM((seq // CMP_STRIDE, HEAD_DIM), BF16),
            pltpu.VMEM((seq, HEAD_DIM + LANES), BF16),
            pltpu.VMEM((seq, HEAD_DIM + LANES), BF16),
            pltpu.VMEM((2, NSA_HPG * tq, HEAD_DIM + LANES), BF16),
            pltpu.VMEM((NSA_HPG, tq, HEAD_DIM), F32),
            pltpu.VMEM((seq, HEAD_DIM + LANES), BF16),
            pltpu.VMEM((seq, HEAD_DIM + LANES), BF16),
        ],
        compiler_params=_cparams(("parallel", "parallel", "arbitrary")),
    )(slopes, z4, z4, z4, z4, z4, z4, z4, gates, q_gain, k_gain, w1, b1, w2, ovt, ktab, ctab)


DIL_TQ = 256


def _dil_kernel(slope_ref, *refs):
    n_pat = len(DIL_PATTERNS)
    qkv_refs = refs[:3 * n_pat]
    qg_ref, kg_ref, o_ref, qs, ks, vs, num_s, m_s, den_s = refs[3 * n_pat:]
    h = pl.program_id(1)
    seq = qs.shape[0]

    for p, (w, r) in enumerate(DIL_PATTERNS):
        assert w // r == DIL_SPAN
        q_ref, k_ref, v_ref = qkv_refs[3 * p:3 * p + 3]
        qs[...] = _rms(q_ref[0, 0].astype(F32), qg_ref[...])
        ks[...] = _rms(k_ref[0, 0].astype(F32), kg_ref[...])
        vs[...] = v_ref[0, 0].astype(F32)
        slope_r = slope_ref[DIL_HPG * p + h] * float(r)
        length = seq // r
        tq = min(DIL_TQ, length)
        nk = min(length, tq + DIL_SPAN)
        cpt = DIL_TQ // tq
        assert length & (length - 1) == 0 and r % cpt == 0 and (cpt == 1 or nk == length)

        def rows(start, size, r=r):
            return pl.ds(start, size) if r == 1 else pl.ds(start, size, stride=r)

        def tile(idx, carry, p=p, r=r, tq=tq, nk=nk, cpt=cpt, length=length, slope_r=slope_r, rows=rows):
            if cpt == 1:
                c = idx // (length // tq)
                u0 = (idx % (length // tq)) * tq
                k0 = jnp.maximum(u0 - DIL_SPAN, 0)
                rqs, rks = [rows(c + u0 * r, tq)], [rows(c + k0 * r, nk)]
            else:
                u0 = k0 = 0
                rqs = rks = [rows(idx * cpt + j, length) for j in range(cpt)]
            gather = lambda ref, rr: jnp.concatenate([ref[x, :] for x in rr], axis=0).astype(BF16)
            q, k, v = gather(qs, rqs), gather(ks, rks), gather(vs, rks)
            i_row = lax.broadcasted_iota(jnp.int32, (cpt * tq, 1), 0)
            i_col = lax.broadcasted_iota(jnp.int32, (1, cpt * nk), 1)
            du = (u0 + (i_row & (length - 1))) - (k0 + (i_col & (length - 1)))
            s = _dot_nt(q, k) * SCALE - slope_r * du.astype(F32)
            s = jnp.where(du >= 0, jnp.where(du <= DIL_SPAN, s, NEG_INF), NEG_INF)
            if cpt > 1:
                s = jnp.where(((i_row ^ i_col) & -length) == 0, s, NEG_INF)
            m = jnp.max(s, axis=-1, keepdims=True)
            e = jnp.exp(s - m)
            den = jnp.sum(e, axis=-1, keepdims=True)
            num = _dot(e.astype(BF16), v)
            for j, rq in enumerate(rqs):
                sl = slice(j * tq, (j + 1) * tq)
                num_s[p, rq, :] = num[sl]
                m_s[p, rq, :] = jnp.broadcast_to(m[sl], (tq, HEAD_DIM))
                den_s[p, rq, :] = jnp.broadcast_to(den[sl], (tq, HEAD_DIM))
            return carry

        lax.fori_loop(0, r * (length // tq) // cpt, tile, 0, unroll=2)

    m_all = [m_s[p] for p in range(n_pat)]
    m_max = functools.reduce(jnp.maximum, m_all)
    numer = jnp.zeros((seq, HEAD_DIM), F32)
    denom = jnp.zeros((seq, HEAD_DIM), F32)
    for p in range(n_pat):
        wt = jnp.exp(m_all[p] - m_max)
        numer = numer + wt * num_s[p]
        denom = denom + wt * den_s[p]
    o_ref[...] = (numer / denom).astype(BF16)


def _dil(z4, q_gain, k_gain, slopes):
    _, bsz, seq, _ = z4.shape
    n = bsz * seq
    n_pat = len(DIL_PATTERNS)

    def spec(which, p):
        base = DIL_SLAB0 + which * DIL_HEADS + DIL_HPG * p
        return pl.BlockSpec((1, 1, seq, HEAD_DIM), lambda b, h: (base + h, b, 0, 0))

    qkv_specs = [spec(which, p) for p in range(n_pat) for which in range(3)]
    gain = pl.BlockSpec((1, HEAD_DIM), lambda b, h: (0, 0))
    return pl.pallas_call(
        _dil_kernel,
        grid=(bsz, DIL_HPG),
        in_specs=[pl.BlockSpec(memory_space=pltpu.SMEM)] + qkv_specs + [gain, gain],
        out_specs=pl.BlockSpec((seq, HEAD_DIM), lambda b, h: (b, h)),
        out_shape=jax.ShapeDtypeStruct((n, DIL_HPG * HEAD_DIM), BF16),
        scratch_shapes=[pltpu.VMEM((seq, HEAD_DIM), F32)] * 3
        + [pltpu.VMEM((n_pat, seq, HEAD_DIM), F32)] * 3,
        compiler_params=_cparams(("parallel", "parallel")),
    )(slopes, *([z4] * (3 * n_pat)), q_gain, k_gain)


def _pool_kernel(u_ref, w_ref, sc_ref, o_ref):
    seq = u_ref.shape[2]
    t = lax.broadcasted_iota(jnp.int32, (seq, 1), 0)
    for g, w in enumerate(POOL_WINDOWS):
        u = u_ref[g, 0].astype(F32)
        acc = u
        span = 1
        while span < w:
            acc = acc + jnp.where(t >= span, pltpu.roll(acc, span, axis=0), 0.0)
            span *= 2
        d = acc / jnp.minimum(t + 1, w).astype(F32) - u
        y = _dot(d.astype(BF16), w_ref[g]) * sc_ref[:, g * LANES:(g + 1) * LANES]
        o_ref[:, g * LANES:(g + 1) * LANES] = y.astype(BF16)


def _pool(z4, w_pool, scale):
    _, bsz, seq, _ = z4.shape
    ng = len(POOL_WINDOWS)
    assert all(w & (w - 1) == 0 for w in POOL_WINDOWS) and POOL_SLAB0 % ng == 0
    return pl.pallas_call(
        _pool_kernel,
        grid=(bsz,),
        in_specs=[
            pl.BlockSpec((ng, 1, seq, LANES), lambda b: (POOL_SLAB0 // ng, b, 0, 0)),
            pl.BlockSpec((ng, LANES, LANES), lambda b: (0, 0, 0)),
            pl.BlockSpec((1, ng * LANES), lambda b: (0, 0)),
        ],
        out_specs=pl.BlockSpec((seq, ng * LANES), lambda b: (b, 0)),
        out_shape=jax.ShapeDtypeStruct((bsz * seq, ng * LANES), BF16),
        compiler_params=_cparams(("parallel",)),
    )(z4, w_pool, scale)


MERGE_TN = 512


def _merge_out_kernel(x_ref, ya_ref, yb_ref, yc_ref, g_ref, wa_ref, wb_ref, wc_ref, wo_ref, o_ref, m_ref):
    d = o_ref.shape[1]
    n_gate = d // LANES
    ya, yb, yc = ya_ref[...], yb_ref[...], yc_ref[...]
    for j in range(d // MERGE_TN):
        cols = slice(j * MERGE_TN, (j + 1) * MERGE_TN)
        a = _dot(ya, wa_ref[:, cols])
        b = _dot(yb, wb_ref[:, cols])
        c = _dot(yc, wc_ref[:, cols])
        for s in range(MERGE_TN // LANES):
            sl = slice(s * LANES, (s + 1) * LANES)
            slab = j * (MERGE_TN // LANES) + s
            m_ref[:, slab * LANES:(slab + 1) * LANES] = (
                g_ref[slab].astype(F32) * a[:, sl] + g_ref[n_gate + slab].astype(F32) * b[:, sl]
                + g_ref[2 * n_gate + slab].astype(F32) * c[:, sl]).astype(BF16)
    o_ref[...] = x_ref[...] + _dot(m_ref[...], wo_ref[...])


def _merge_out(x2, ya, yb, yc, wa, wb, wc, w_out, z, layer, tm=512):
    n, d = x2.shape
    act = lambda a: pl.BlockSpec((tm, a.shape[1]), lambda i: (i, 0))
    wgt = lambda w: pl.BlockSpec((None,) + w.shape[1:], lambda i: (layer, 0, 0), pipeline_mode=pl.Buffered(1))
    return pl.pallas_call(
        _merge_out_kernel,
        grid=(n // tm,),
        in_specs=[act(x2), act(ya), act(yb), act(yc),
                  pl.BlockSpec((MERGE_SLABS, tm, LANES), lambda i: (0, i, 0)),
                  wgt(wa), wgt(wb), wgt(wc), wgt(w_out)],
        out_specs=pl.BlockSpec((tm, d), lambda i: (i, 0)),
        out_shape=jax.ShapeDtypeStruct((n, d), F32),
        scratch_shapes=[pltpu.VMEM((tm, d), BF16)],
        compiler_params=_cparams(("parallel",)),
    )(x2, ya, yb, yc, z, wa, wb, wc, w_out)


FFN_TF = 512


def _ffn_kernel(x_ref, g_ref, wg_ref, wu_ref, wd_ref, o_ref, h_ref):
    f = pl.program_id(1)

    @pl.when(f == 0)
    def _():
        x = x_ref[...]
        h_ref[...] = _rms(x, g_ref[...]).astype(BF16)
        o_ref[...] = x

    h = h_ref[...]
    a = (jax.nn.silu(_dot(h, wg_ref[...])) * _dot(h, wu_ref[...])).astype(BF16)
    o_ref[...] += _dot(a, wd_ref[...])


def _ffn(x2, g, w_gate, w_up, w_down, layer, tm=512):
    n, d = x2.shape
    d_ff = w_gate.shape[2]
    return pl.pallas_call(
        _ffn_kernel,
        grid=(n // tm, d_ff // FFN_TF),
        in_specs=[
            pl.BlockSpec((tm, d), lambda i, f: (i, 0)),
            pl.BlockSpec((1, d), lambda i, f: (0, 0)),
            pl.BlockSpec((None, d, FFN_TF), lambda i, f: (layer, 0, f)),
            pl.BlockSpec((None, d, FFN_TF), lambda i, f: (layer, 0, f)),
            pl.BlockSpec((None, FFN_TF, d), lambda i, f: (layer, f, 0)),
        ],
        out_specs=pl.BlockSpec((tm, d), lambda i, f: (i, 0)),
        out_shape=jax.ShapeDtypeStruct((n, d), F32),
        scratch_shapes=[pltpu.VMEM((tm, d), BF16)],
        compiler_params=_cparams(("parallel", "arbitrary")),
    )(x2, g, w_gate, w_up, w_down)


def _router_kernel(x_ref, g_ref, w_ref, b_ref, idx_ref, wt_ref):
    h = _rms(x_ref[...], g_ref[...])
    h_hi = h.astype(BF16)
    h_lo = (h - h_hi.astype(F32)).astype(BF16)
    logits = _dot(h_hi, w_ref[0]) + (_dot(h_hi, w_ref[1]) + _dot(h_lo, w_ref[0])) + b_ref[...]
    lane = lax.broadcasted_iota(jnp.int32, logits.shape, 1)
    l1 = jnp.where(lane < N_EXPERTS, logits, -jnp.inf)
    m1 = jnp.max(l1, axis=-1, keepdims=True)
    i1 = jnp.min(jnp.where(l1 == m1, lane, LANES), axis=-1, keepdims=True)
    l2 = jnp.where(lane == i1, -jnp.inf, l1)
    m2 = jnp.max(l2, axis=-1, keepdims=True)
    i2 = jnp.min(jnp.where(l2 == m2, lane, LANES), axis=-1, keepdims=True)
    e = jnp.exp(m2 - m1)
    w1 = 1.0 / (1.0 + e)
    idx_ref[...] = jnp.where(lane == 0, i1, jnp.where(lane == 1, i2, 0))
    wt_ref[...] = jnp.where(lane == 0, w1, jnp.where(lane == 1, e * w1, 0.0))


def _router(x2, g, w_r, b_r, tm=512):
    n, d = x2.shape
    return pl.pallas_call(
        _router_kernel,
        grid=(n // tm,),
        in_specs=[
            pl.BlockSpec((tm, d), lambda i: (i, 0)),
            pl.BlockSpec((1, d), lambda i: (0, 0)),
            pl.BlockSpec((2, d, LANES), lambda i: (0, 0, 0)),
            pl.BlockSpec((1, LANES), lambda i: (0, 0)),
        ],
        out_specs=[pl.BlockSpec((tm, LANES), lambda i: (i, 0))] * 2,
        out_shape=[jax.ShapeDtypeStruct((n, LANES), jnp.int32), jax.ShapeDtypeStruct((n, LANES), F32)],
        compiler_params=_cparams(("parallel",)),
    )(x2, g, w_r, b_r)


MOE_TM = 512
MOE_TF = 512


def _moe_kernel(tile_e_ref, n_used_ref, row_tok_ref, x_hbm, g_ref, wg_ref, wu_ref, wd_ref, o_ref,
                xbuf, h_ref, sem):
    t = pl.program_id(0)
    f = pl.program_id(1)
    tm = MOE_TM
    n_used = n_used_ref[0]
    used = t < n_used
    slot = lax.rem(t, 2)

    def row_copy(tile, r, s):
        tok = row_tok_ref[tile * tm + r]
        return pltpu.make_async_copy(x_hbm.at[pl.ds(tok, 1), :], xbuf.at[s, pl.ds(r, 1), :], sem.at[s])

    def start_tile(tile, s):
        lax.fori_loop(0, tm, lambda r, c: (row_copy(tile, r, s).start(), c)[1], 0, unroll=8)

    def wait_tile(tile, s):
        lax.fori_loop(0, tm, lambda r, c: (row_copy(tile, r, s).wait(), c)[1], 0, unroll=8)

    @pl.when(f == 0)
    def _():
        o_ref[...] = jnp.zeros_like(o_ref)

    @pl.when((f == 0) & (t == 0))
    def _():
        start_tile(0, 0)

    @pl.when((f == 0) & used)
    def _():
        wait_tile(t, slot)

        @pl.when(t + 1 < n_used)
        def _():
            start_tile(t + 1, 1 - slot)

        h_ref[...] = _rms(xbuf[slot], g_ref[...]).astype(BF16)

    @pl.when(used)
    def _():
        h = h_ref[...]
        a = (jax.nn.silu(_dot(h, wg_ref[...])) * _dot(h, wu_ref[...])).astype(BF16)
        o_ref[...] += _dot(a, wd_ref[...])


def _moe_experts(x2, g, w_gate, w_up, w_down, layer, tile_e, n_used, row_tok):
    n, d = x2.shape
    d_ff = w_gate.shape[2]
    n_f = d_ff // MOE_TF
    n_tiles = tile_e.shape[0]
    e0 = layer * N_EXPERTS

    def f_eff(t, f, nu):
        return jnp.where(t < nu[0], f, n_f - 1)

    grid_spec = pltpu.PrefetchScalarGridSpec(
        num_scalar_prefetch=3,
        grid=(n_tiles, n_f),
        in_specs=[
            pl.BlockSpec(memory_space=pl.ANY),
            pl.BlockSpec((1, d), lambda t, f, te, nu, rt: (0, 0)),
            pl.BlockSpec((None, d, MOE_TF), lambda t, f, te, nu, rt: (e0 + te[t], 0, f_eff(t, f, nu))),
            pl.BlockSpec((None, d, MOE_TF), lambda t, f, te, nu, rt: (e0 + te[t], 0, f_eff(t, f, nu))),
            pl.BlockSpec((None, MOE_TF, d), lambda t, f, te, nu, rt: (e0 + te[t], f_eff(t, f, nu), 0)),
        ],
        out_specs=pl.BlockSpec((MOE_TM, d), lambda t, f, te, nu, rt: (t, 0)),
        scratch_shapes=[
            pltpu.VMEM((2, MOE_TM, d), F32),
            pltpu.VMEM((MOE_TM, d), BF16),
            pltpu.SemaphoreType.DMA((2,)),
        ],
    )
    return pl.pallas_call(
        _moe_kernel,
        grid_spec=grid_spec,
        out_shape=jax.ShapeDtypeStruct((n_tiles * MOE_TM, d), F32),
        compiler_params=_cparams(("arbitrary", "arbitrary")),
    )(tile_e, n_used, row_tok, x2, g, w_gate, w_up, w_down)


COMBINE_TM = 256


def _combine_kernel(pos_ref, x_ref, wt_ref, y_hbm, o_ref, buf, sem):
    i = pl.program_id(0)
    tm = COMBINE_TM
    slot = lax.rem(i, 2)

    def row_copy(tile, r, k, s):
        src = pos_ref[(tile * tm + r) * TOP_K + k]
        return pltpu.make_async_copy(y_hbm.at[pl.ds(src, 1), :], buf.at[s, k, pl.ds(r, 1), :], sem.at[s])

    def start_tile(tile, s):
        def body(r, c):
            for k in range(TOP_K):
                row_copy(tile, r, k, s).start()
            return c
        lax.fori_loop(0, tm, body, 0, unroll=8)

    def wait_tile(tile, s):
        def body(r, c):
            for k in range(TOP_K):
                row_copy(tile, r, k, s).wait()
            return c
        lax.fori_loop(0, tm, body, 0, unroll=8)

    @pl.when(i == 0)
    def _():
        start_tile(0, 0)

    wait_tile(i, slot)

    @pl.when(i + 1 < pl.num_programs(0))
    def _():
        start_tile(i + 1, 1 - slot)

    wt = wt_ref[...]
    acc = x_ref[...]
    for k in range(TOP_K):
        acc = acc + wt[:, k:k + 1] * buf[slot, k]
    o_ref[...] = acc


def _combine(x2, wts, ys, pos):
    n, d = x2.shape
    grid_spec = pltpu.PrefetchScalarGridSpec(
        num_scalar_prefetch=1,
        grid=(n // COMBINE_TM,),
        in_specs=[
            pl.BlockSpec((COMBINE_TM, d), lambda i, p: (i, 0)),
            pl.BlockSpec((COMBINE_TM, LANES), lambda i, p: (i, 0)),
            pl.BlockSpec(memory_space=pl.ANY),
        ],
        out_specs=pl.BlockSpec((COMBINE_TM, d), lambda i, p: (i, 0)),
        scratch_shapes=[pltpu.VMEM((2, TOP_K, COMBINE_TM, d), F32), pltpu.SemaphoreType.DMA((2,))],
    )
    return pl.pallas_call(
        _combine_kernel,
        grid_spec=grid_spec,
        out_shape=jax.ShapeDtypeStruct((n, d), F32),
        compiler_params=_cparams(("arbitrary",)),
    )(pos, x2, wts, ys)


def _route_plan(idx, n):
    flat_e = idx[:, :TOP_K].reshape(-1)
    onehot = (flat_e[:, None] == jnp.arange(N_EXPERTS, dtype=jnp.int32)[None, :]).astype(jnp.int32)
    before = jnp.cumsum(onehot, axis=0) - onehot
    rank = jnp.sum(before * onehot, axis=1)
    counts = jnp.sum(onehot, axis=0)
    padded = ((counts + MOE_TM - 1) // MOE_TM) * MOE_TM
    ends = jnp.cumsum(padded)
    pos = (ends - padded)[flat_e] + rank
    n_rows = TOP_K * n + N_EXPERTS * MOE_TM
    row_tok = jnp.zeros((n_rows,), jnp.int32).at[pos].set(jnp.arange(TOP_K * n, dtype=jnp.int32) // TOP_K)
    tile_start = jnp.arange(n_rows // MOE_TM, dtype=jnp.int32) * MOE_TM
    tile_e = jnp.sum((tile_start[:, None] >= ends[None, :]).astype(jnp.int32), axis=1)
    tile_e = jnp.minimum(tile_e, N_EXPERTS - 1)
    n_used = (ends[-1] // MOE_TM).astype(jnp.int32).reshape(1)
    return tile_e, n_used, row_tok, pos.astype(jnp.int32)


def _moe(x2, g, w_r, b_r, w_gate, w_up, w_down, layer):
    n = x2.shape[0]
    idx, wts = _router(x2, g, w_r, b_r)
    tile_e, n_used, row_tok, pos = _route_plan(idx, n)
    ys = _moe_experts(x2, g, w_gate, w_up, w_down, layer, tile_e, n_used, row_tok)
    return _combine(x2, wts, ys, pos)


def _prep_in_proj(w_in):
    d_model = w_in.shape[1]
    q_w = NSA_HEADS * HEAD_DIM
    kv_w = NSA_BRANCHES * 2 * NSA_KV_GROUPS * HEAD_DIM
    gate_w = NSA_BRANCHES * NSA_HEADS
    dil_w = 3 * DIL_HEADS * HEAD_DIM
    pool_w = len(POOL_WINDOWS) * LANES
    o_q, o_kv, o_g = 0, q_w, q_w + kv_w
    o_dil = o_g + gate_w
    o_pool = o_dil + dil_w
    o_merge = o_pool + pool_w
    assert w_in.shape[2] == o_merge + 3 * d_model
    w_bf = w_in.astype(BF16)
    w_main = jnp.concatenate(
        [w_bf[:, :, o_merge:], w_bf[:, :, o_q:o_g], w_bf[:, :, o_dil:o_merge]], axis=2)
    assert w_main.shape[2] == N_SLABS * LANES
    groups = []
    for g in range(NSA_KV_GROUPS):
        parts = []
        for br in range(NSA_BRANCHES):
            c0 = o_g + br * NSA_HEADS + g * NSA_HPG
            parts.append(w_bf[:, :, c0:c0 + NSA_HPG])
        parts.append(jnp.zeros(w_bf.shape[:2] + (LANES - NSA_BRANCHES * NSA_HPG,), BF16))
        groups.append(jnp.concatenate(parts, axis=2))
    w_gate = jnp.stack(groups, axis=1)
    return w_main, w_gate


def _sel_tables(seq):
    n_cmp = seq // CMP_STRIDE - 1
    n_sel = seq // SEL_BLOCK
    i = np.arange(n_cmp)[:, None] * CMP_STRIDE
    j = np.arange(n_sel)[None, :] * SEL_BLOCK
    ov = np.clip(np.minimum(i + CMP_BLOCK, j + SEL_BLOCK) - np.maximum(i, j), 0, None) / CMP_BLOCK
    ovt = np.zeros((n_sel, n_cmp + 1), np.float32)
    ovt[:, :n_cmp] = ov.T
    assert NSA_SEL_LANE0 + n_sel <= LANES
    ktab = _key_table(np.arange(seq), SEL_BLOCK)
    ctab = _key_table(np.arange(n_cmp + 1) * CMP_STRIDE + CMP_BLOCK - 1)
    return jnp.asarray(ovt), jnp.asarray(ktab, dtype=BF16), jnp.asarray(ctab, dtype=BF16)


def kernel(x, mix_norm, w_in, nsa_q_norm, nsa_k_norm, cmp_w1, cmp_b1, cmp_w2, dil_q_norm, dil_k_norm, pool_w, pool_scale, w_branch_a, w_branch_b, w_branch_c, w_out, ffn_norm, ffn_w_gate, ffn_w_up, ffn_w_down, moe_w_router, moe_b_router, moe_w_gate, moe_w_up, moe_w_down):
    bsz, seq, d_model = x.shape
    depth = w_in.shape[0]
    n = bsz * seq
    nsa_slopes = _alibi_slopes(NSA_HEADS)
    dil_slopes = _alibi_slopes(DIL_HEADS)
    ovt, ktab, ctab = _sel_tables(seq)
    x2 = x.reshape(n, d_model)
    w_main, w_gate = _prep_in_proj(w_in)
    wb_a, wb_b, wb_c, wb_out = (w.astype(BF16) for w in (w_branch_a, w_branch_b, w_branch_c, w_out))
    wf_g, wf_u, wf_d = (w.astype(BF16) for w in (ffn_w_gate, ffn_w_up, ffn_w_down))
    wm_g, wm_u, wm_d = (w.astype(BF16).reshape((-1,) + w.shape[2:]) for w in (moe_w_gate, moe_w_up, moe_w_down))
    cw1, cw2, pw = cmp_w1.astype(BF16), cmp_w2.astype(BF16), pool_w.astype(BF16)
    for l in range(depth):
        z, gates = _in_proj(x2, mix_norm[l][None, :], w_main, w_gate, l)
        z4 = z.reshape(N_SLABS, bsz, seq, LANES)
        y_a = _nsa(z4, gates, nsa_q_norm[l][None, :], nsa_k_norm[l], cw1[l], cmp_b1[l], cw2[l],
                   nsa_slopes, ovt, ktab, ctab)
        y_b = _dil(z4, dil_q_norm[l][None, :], dil_k_norm[l][None, :], dil_slopes)
        y_c = _pool(z4, pw[l], pool_scale[l][None, :])
        x2 = _merge_out(x2, y_a, y_b, y_c, wb_a, wb_b, wb_c, wb_out, z, l)
        i = l // 2
        if l % 2 == 0:
            x2 = _ffn(x2, ffn_norm[l][None, :], wf_g, wf_u, wf_d, i)
        else:
            w_r = jnp.zeros((d_model, LANES), F32).at[:, :N_EXPERTS].set(moe_w_router[i])
            w_hi = w_r.astype(BF16)
            w_r = jnp.stack([w_hi, (w_r - w_hi.astype(F32)).astype(BF16)])
            b_r = jnp.zeros((1, LANES), F32).at[0, :N_EXPERTS].set(moe_b_router[i])
            x2 = _moe(x2, ffn_norm[l][None, :], w_r, b_r, wm_g, wm_u, wm_d, i)
    return x2.reshape(bsz, seq, d_model)
```

```python
import functools

import numpy as np
import jax
import jax.numpy as jnp
from jax import lax
from jax.experimental import pallas as pl
from jax.experimental.pallas import tpu as pltpu

F32 = jnp.float32
BF16 = jnp.bfloat16

HEAD_DIM = 128
NORM_EPS = 1e-6
NEG_INF = -1e30
NSA_HEADS = 6
NSA_KV_GROUPS = 2
NSA_HPG = NSA_HEADS // NSA_KV_GROUPS
NSA_BRANCHES = 3
CMP_BLOCK = 32
CMP_STRIDE = 16
SEL_BLOCK = 64
SEL_TOP = 8
SEL_FORCE = 1e4
WIN = 512
DIL_PATTERNS = ((128, 1), (512, 4), (2048, 16))
DIL_HPG = 2
DIL_HEADS = DIL_HPG * len(DIL_PATTERNS)
DIL_SPAN = 128
POOL_WINDOWS = (2, 4, 8, 16)
N_EXPERTS = 8
TOP_K = 2
SCALE = HEAD_DIM ** -0.5

LANES = 128
MERGE_SLABS = 3 * 16
Q_SLAB0 = MERGE_SLABS
KV_SLAB0 = Q_SLAB0 + NSA_HEADS
DIL_SLAB0 = KV_SLAB0 + NSA_BRANCHES * 2 * NSA_KV_GROUPS
POOL_SLAB0 = DIL_SLAB0 + 3 * DIL_HEADS
N_SLABS = POOL_SLAB0 + len(POOL_WINDOWS)

VMEM_LIMIT = 56 * 1024 * 1024


def _cparams(sem):
    return pltpu.CompilerParams(dimension_semantics=sem, vmem_limit_bytes=VMEM_LIMIT)


def _rms(x, g):
    return x * lax.rsqrt(jnp.mean(x * x, axis=-1, keepdims=True) + NORM_EPS) * g


def _dot(a, b):
    return jnp.dot(a, b, preferred_element_type=F32)


def _dot_nt(a, b):
    return lax.dot_general(a, b, (((1,), (1,)), ((), ())), preferred_element_type=F32)


def _alibi_slopes(n):
    return jnp.asarray(2.0 ** (-8.0 * np.arange(1, n + 1) / n), dtype=F32)


IN_TN = 1024
IN_SLABS_PER_TILE = IN_TN // LANES
IN_SIG_TILES = MERGE_SLABS // IN_SLABS_PER_TILE


def _in_proj_kernel(x_ref, g_ref, w_ref, wg_ref, z_ref, gate_ref, h_ref):
    j = pl.program_id(1)

    @pl.when(j == 0)
    def _():
        h = _rms(x_ref[...], g_ref[...]).astype(BF16)
        h_ref[...] = h
        for g in range(NSA_KV_GROUPS):
            gate_ref[g] = jax.nn.sigmoid(_dot(h, wg_ref[g]))

    acc = _dot(h_ref[...], w_ref[...])
    is_gate = j < IN_SIG_TILES
    for c in range(IN_SLABS_PER_TILE):
        a = acc[:, c * LANES:(c + 1) * LANES]
        z_ref[c] = jnp.where(is_gate, jax.nn.sigmoid(a), a).astype(BF16)


def _in_proj(x2, g, w_main, w_gate, layer, tm=1024):
    n, d = x2.shape
    n_tiles = w_main.shape[2] // IN_TN
    return pl.pallas_call(
        _in_proj_kernel,
        grid=(n // tm, n_tiles),
        in_specs=[
            pl.BlockSpec((tm, d), lambda i, j: (i, 0)),
            pl.BlockSpec((1, d), lambda i, j: (0, 0)),
            pl.BlockSpec((None, d, IN_TN), lambda i, j: (layer, 0, j)),
            pl.BlockSpec((None, NSA_KV_GROUPS, d, LANES), lambda i, j: (layer, 0, 0, 0)),
        ],
        out_specs=[
            pl.BlockSpec((IN_SLABS_PER_TILE, tm, LANES), lambda i, j: (j, i, 0)),
            pl.BlockSpec((NSA_KV_GROUPS, tm, LANES), lambda i, j: (0, i, 0)),
        ],
        out_shape=[
            jax.ShapeDtypeStruct((N_SLABS, n, LANES), BF16),
            jax.ShapeDtypeStruct((NSA_KV_GROUPS, n, LANES), F32),
        ],
        scratch_shapes=[pltpu.VMEM((tm, d), BF16)],
        compiler_params=_cparams(("parallel", "arbitrary")),
    )(x2, g, w_main, w_gate)


NSA_TQ = 512
LOG2E = 1.4426950408889634
NSA_POS_SPLIT = 64
NSA_SEL_LANE0 = 8
NSA_BIG = 2.0 ** 100


def _key_table(pos, block=None):
    tab = np.zeros((len(pos), LANES), np.float32)
    tab[:, 0:3] = (pos // NSA_POS_SPLIT * NSA_POS_SPLIT)[:, None]
    tab[:, 3:6] = (pos % NSA_POS_SPLIT)[:, None]
    if block is not None:
        tab[np.arange(len(pos)), NSA_SEL_LANE0 + pos // block] = 1.0
    return tab


def _slope_row(slope):
    lane = lax.broadcasted_iota(jnp.int32, (1, LANES), 1)
    sv = jnp.full((1, LANES), slope, F32)
    hi = sv.astype(BF16).astype(F32)
    mid = (sv - hi).astype(BF16).astype(F32)
    lo = sv - hi - mid
    part = jnp.where((lane == 0) | (lane == 3), hi, jnp.where((lane == 1) | (lane == 4), mid, lo))
    return jnp.where(lane < 6, part, 0.0)


def _nsa_kernel(slope_ref, q_ref, kc_ref, vc_ref, ks_ref, vs_ref, kw_ref, vw_ref, gate_ref,
                qg_ref, kg_ref, w1_ref, b1_ref, w2_ref, ovt_ref, ktab_ref, ctab_ref, o_ref,
                tmp32, kca_s, vcn_s, ksa_s, kwa_s, qa_s, osel_s, vsa_s, vwa_s):
    g = pl.program_id(1)
    qi = pl.program_id(2)
    tq = NSA_TQ
    seq = ks_ref.shape[2]
    n_cmp_pad = seq // CMP_STRIDE
    n_sel = seq // SEL_BLOCK
    hd = HEAD_DIM

    @pl.when(qi == 0)
    def _():
        for which, src in enumerate((kc_ref, vc_ref)):
            tmp32[...] = src[0, 0].astype(F32)
            lo = jnp.zeros((n_cmp_pad, HEAD_DIM), F32)
            hi = jnp.zeros((n_cmp_pad, HEAD_DIM), F32)
            for l in range(CMP_STRIDE):
                xl = tmp32[pl.ds(l, n_cmp_pad, stride=CMP_STRIDE), :].astype(BF16)
                lo = lo + _dot(xl, w1_ref[which, l])
                hi = hi + _dot(xl, w1_ref[which, CMP_STRIDE + l])
            hid = jax.nn.gelu(lo + pltpu.roll(hi, n_cmp_pad - 1, axis=0) + b1_ref[which:which + 1, :])
            c = _dot(hid.astype(BF16), w2_ref[which])
            if which == 0:
                kca_s[:, :hd] = _rms(c, kg_ref[0:1, :]).astype(BF16)
                kca_s[:, hd:] = ctab_ref[...]
            else:
                vcn_s[...] = c.astype(BF16)
        ksa_s[:, :hd] = _rms(ks_ref[0, 0].astype(F32), kg_ref[1:2, :]).astype(BF16)
        ksa_s[:, hd:] = ktab_ref[...]
        kwa_s[:, :hd] = _rms(kw_ref[0, 0].astype(F32), kg_ref[2:3, :]).astype(BF16)
        kwa_s[:, hd:] = ktab_ref[...]
        for v_ref, va_s in ((vs_ref, vsa_s), (vw_ref, vwa_s)):
            va_s[:, :hd] = v_ref[0, 0]
            va_s[:, hd:] = jnp.ones((seq, LANES), BF16)

    q0 = qi * tq
    t_col = q0 + lax.broadcasted_iota(jnp.int32, (tq, 1), 0)
    rows_of = lambda hh: slice(hh * tq, (hh + 1) * tq)
    slope_rows = []
    for hh in range(NSA_HPG):
        qs = (_rms(q_ref[hh, 0].astype(F32), qg_ref[...]) * (SCALE * LOG2E)).astype(BF16)
        slope_rows.append(_slope_row(slope_ref[g * NSA_HPG + hh] * LOG2E))
        qa_s[0, rows_of(hh), :hd] = qs
        qa_s[1, rows_of(hh), :hd] = qs
        qa_s[0, rows_of(hh), hd:] = jnp.broadcast_to(slope_rows[hh].astype(BF16), (tq, LANES))

    def stack(x):
        return jnp.concatenate([x] * NSA_HPG, axis=0)

    n_row = lax.broadcasted_iota(jnp.int32, (1, n_cmp_pad), 1)
    c_end = n_row * CMP_STRIDE + (CMP_BLOCK - 1)
    valid_c = (c_end <= t_col) & (n_row < n_cmp_pad - 1)
    s = _dot_nt(qa_s[0], kca_s[...]) + stack(jnp.where(valid_c, 0.0, NEG_INF))
    e = jnp.exp2(s - jnp.max(s, axis=-1, keepdims=True))
    p = e * (stack(valid_c.astype(F32)) / jnp.sum(e, axis=-1, keepdims=True))
    o_cmp = _dot(p.astype(BF16), vcn_s[...])
    psum = functools.reduce(lambda a, b: a + b, [p[rows_of(hh)] for hh in range(NSA_HPG)])

    imp_t = lax.dot_general(ovt_ref[...], psum, (((1,), (1,)), ((), ())),
                            precision=lax.Precision.HIGHEST, preferred_element_type=F32)
    blk = lax.broadcasted_iota(jnp.int32, (n_sel, tq), 0)
    cur = lax.shift_right_logical(q0 + lax.broadcasted_iota(jnp.int32, (n_sel, tq), 1),
                                  int(np.log2(SEL_BLOCK)))
    forced = (blk == 0) | (blk == cur) | (blk == cur - 1)
    imp = jnp.where(forced, SEL_FORCE, jnp.where(blk <= cur, imp_t, -1.0))
    rank = jnp.zeros((n_sel, tq), jnp.int32)
    for i in range(n_sel):
        ri = imp[i:i + 1, :]
        beats = (ri > imp) | ((ri == imp) & (blk > i))
        rank = rank + beats.astype(jnp.int32)
    sel_lanes = jnp.where(rank < min(SEL_TOP, n_sel), 0.0, -NSA_BIG)
    sel_lanes = jnp.concatenate(
        [jnp.zeros((NSA_SEL_LANE0, tq), F32), sel_lanes,
         jnp.zeros((LANES - NSA_SEL_LANE0 - n_sel, tq), F32)], axis=0).T
    for hh in range(NSA_HPG):
        qa_s[1, rows_of(hh), hd:] = (slope_rows[hh] + sel_lanes).astype(BF16)

    def attend(q, k, v, mask):
        s = _dot_nt(q, k) + mask
        e = jnp.exp2(s - jnp.max(s, axis=-1, keepdims=True))
        o = _dot(e.astype(BF16), v)
        return o[:, :hd] / o[:, hd:]

    k_pos = lax.broadcasted_iota(jnp.int32, (1, seq), 1)
    s_mask = jnp.where(k_pos <= t_col, 0.0, NEG_INF)

    w_len = WIN + tq
    w0 = pl.multiple_of(jnp.maximum(q0 - WIN, 0), tq)
    w_dist = t_col - (w0 + lax.broadcasted_iota(jnp.int32, (1, w_len), 1))
    w_mask = jnp.where(w_dist >= 0, jnp.where(w_dist < WIN, 0.0, NEG_INF), NEG_INF)
    kwin = kwa_s[pl.ds(w0, w_len), :]
    vwin = vwa_s[pl.ds(w0, w_len), :]

    nq = seq // tq
    for lo, hi in ((i, i + 1) for i in range(nq)):
        @pl.when((qi >= lo) & (qi < hi))
        def _(nk=hi * tq):
            for hh in range(NSA_HPG):
                osel_s[hh] = attend(qa_s[1, rows_of(hh)], ksa_s[:nk, :], vsa_s[:nk, :], s_mask[:, :nk])

    gates = gate_ref[0]
    for hh in range(NSA_HPG):
        o_win = attend(qa_s[0, rows_of(hh)], kwin, vwin, w_mask)
        out = (gates[:, hh:hh + 1] * o_cmp[rows_of(hh)]
               + gates[:, NSA_HPG + hh:NSA_HPG + hh + 1] * osel_s[hh]
               + gates[:, 2 * NSA_HPG + hh:2 * NSA_HPG + hh + 1] * o_win)
        o_ref[:, hh * HEAD_DIM:(hh + 1) * HEAD_DIM] = out.astype(BF16)


def _nsa(z4, gates, q_gain, k_gain, w1, b1, w2, slopes, ovt, ktab, ctab):
    _, bsz, seq, _ = z4.shape
    tq = NSA_TQ
    nq = seq // tq
    n = bsz * seq

    def kv_spec(br, kvi):
        base = KV_SLAB0 + br * 2 * NSA_KV_GROUPS + kvi * NSA_KV_GROUPS
        return pl.BlockSpec((1, 1, seq, HEAD_DIM), lambda b, g, qi: (base + g, b, 0, 0))

    full = lambda shape: pl.BlockSpec(shape, lambda b, g, qi: (0,) * len(shape))
    return pl.pallas_call(
        _nsa_kernel,
        grid=(bsz, NSA_KV_GROUPS, nq),
        in_specs=[
            pl.BlockSpec(memory_space=pltpu.SMEM),
            pl.BlockSpec((NSA_HPG, 1, tq, HEAD_DIM), lambda b, g, qi: (Q_SLAB0 // NSA_HPG + g, b, qi, 0)),
            kv_spec(0, 0), kv_spec(0, 1), kv_spec(1, 0), kv_spec(1, 1), kv_spec(2, 0), kv_spec(2, 1),
            pl.BlockSpec((1, tq, LANES), lambda b, g, qi: (g, b * nq + qi, 0)),
            full((1, HEAD_DIM)), full((NSA_BRANCHES, HEAD_DIM)),
            full((2, CMP_BLOCK, HEAD_DIM, HEAD_DIM)), full((2, HEAD_DIM)), full((2, HEAD_DIM, HEAD_DIM)),
            full(ovt.shape), full(ktab.shape), full(ctab.shape),
        ],
        out_specs=pl.BlockSpec((tq, NSA_HPG * HEAD_DIM), lambda b, g, qi: (b * nq + qi, g)),
        out_shape=jax.ShapeDtypeStruct((n, NSA_HEADS * HEAD_DIM), BF16),
        scratch_shapes=[
            pltpu.VMEM((seq, HEAD_DIM), F32),
            pltpu.VMEM((seq // CMP_STRIDE, HEAD_DIM + LANES), BF16),
            pltpu.VMEM((seq // CMP_STRIDE, HEAD_DIM), BF16),
            pltpu.VMEM((seq, HEAD_DIM + LANES), BF16),
            pltpu.VMEM((seq, HEAD_DIM + LANES), BF16),
            pltpu.VMEM((2, NSA_HPG * tq, HEAD_DIM + LANES), BF16),
            pltpu.VMEM((NSA_HPG, tq, HEAD_DIM), F32),
            pltpu.VMEM((seq, HEAD_DIM + LANES), BF16),
            pltpu.VMEM((seq, HEAD_DIM + LANES), BF16),
        ],
        compiler_params=_cparams(("parallel", "parallel", "arbitrary")),
    )(slopes, z4, z4, z4, z4, z4, z4, z4, gates, q_gain, k_gain, w1, b1, w2, ovt, ktab, ctab)


DIL_TQ = 256


def _dil_kernel(slope_ref, *refs):
    n_pat = len(DIL_PATTERNS)
    qkv_refs = refs[:3 * n_pat]
    qg_ref, kg_ref, o_ref, qs, ks, vs, num_s, m_s, den_s = refs[3 * n_pat:]
    h = pl.program_id(1)
    seq = qs.shape[0]

    for p, (w, r) in enumerate(DIL_PATTERNS):
        assert w // r == DIL_SPAN
        q_ref, k_ref, v_ref = qkv_refs[3 * p:3 * p + 3]
        qs[...] = _rms(q_ref[0, 0].astype(F32), qg_ref[...])
        ks[...] = _rms(k_ref[0, 0].astype(F32), kg_ref[...])
        vs[...] = v_ref[0, 0].astype(F32)
        slope_r = slope_ref[DIL_HPG * p + h] * float(r)
        length = seq // r
        tq = min(DIL_TQ, length)
        nk = min(length, tq + DIL_SPAN)
        cpt = DIL_TQ // tq
        assert length & (length - 1) == 0 and r % cpt == 0 and (cpt == 1 or nk == length)

        def rows(start, size, r=r):
            return pl.ds(start, size) if r == 1 else pl.ds(start, size, stride=r)

        def tile(idx, carry, p=p, r=r, tq=tq, nk=nk, cpt=cpt, length=length, slope_r=slope_r, rows=rows):
            if cpt == 1:
                c = idx // (length // tq)
                u0 = (idx % (length // tq)) * tq
                k0 = jnp.maximum(u0 - DIL_SPAN, 0)
                rqs, rks = [rows(c + u0 * r, tq)], [rows(c + k0 * r, nk)]
            else:
                u0 = k0 = 0
                rqs = rks = [rows(idx * cpt + j, length) for j in range(cpt)]
            gather = lambda ref, rr: jnp.concatenate([ref[x, :] for x in rr], axis=0).astype(BF16)
            q, k, v = gather(qs, rqs), gather(ks, rks), gather(vs, rks)
            i_row = lax.broadcasted_iota(jnp.int32, (cpt * tq, 1), 0)
            i_col = lax.broadcasted_iota(jnp.int32, (1, cpt * nk), 1)
            du = (u0 + (i_row & (length - 1))) - (k0 + (i_col & (length - 1)))
            s = _dot_nt(q, k) * SCALE - slope_r * du.astype(F32)
            s = jnp.where(du >= 0, jnp.where(du <= DIL_SPAN, s, NEG_INF), NEG_INF)
            if cpt > 1:
                s = jnp.where(((i_row ^ i_col) & -length) == 0, s, NEG_INF)
            m = jnp.max(s, axis=-1, keepdims=True)
            e = jnp.exp(s - m)
            den = jnp.sum(e, axis=-1, keepdims=True)
            num = _dot(e.astype(BF16), v)
            for j, rq in enumerate(rqs):
                sl = slice(j * tq, (j + 1) * tq)
                num_s[p, rq, :] = num[sl]
                m_s[p, rq, :] = jnp.broadcast_to(m[sl], (tq, HEAD_DIM))
                den_s[p, rq, :] = jnp.broadcast_to(den[sl], (tq, HEAD_DIM))
            return carry

        lax.fori_loop(0, r * (length // tq) // cpt, tile, 0, unroll=2)

    m_all = [m_s[p] for p in range(n_pat)]
    m_max = functools.reduce(jnp.maximum, m_all)
    numer = jnp.zeros((seq, HEAD_DIM), F32)
    denom = jnp.zeros((seq, HEAD_DIM), F32)
    for p in range(n_pat):
        wt = jnp.exp(m_all[p] - m_max)
        numer = numer + wt * num_s[p]
        denom = denom + wt * den_s[p]
    o_ref[...] = (numer / denom).astype(BF16)


def _dil(z4, q_gain, k_gain, slopes):
    _, bsz, seq, _ = z4.shape
    n = bsz * seq
    n_pat = len(DIL_PATTERNS)

    def spec(which, p):
        base = DIL_SLAB0 + which * DIL_HEADS + DIL_HPG * p
        return pl.BlockSpec((1, 1, seq, HEAD_DIM), lambda b, h: (base + h, b, 0, 0))

    qkv_specs = [spec(which, p) for p in range(n_pat) for which in range(3)]
    gain = pl.BlockSpec((1, HEAD_DIM), lambda b, h: (0, 0))
    return pl.pallas_call(
        _dil_kernel,
        grid=(bsz, DIL_HPG),
        in_specs=[pl.BlockSpec(memory_space=pltpu.SMEM)] + qkv_specs + [gain, gain],
        out_specs=pl.BlockSpec((seq, HEAD_DIM), lambda b, h: (b, h)),
        out_shape=jax.ShapeDtypeStruct((n, DIL_HPG * HEAD_DIM), BF16),
        scratch_shapes=[pltpu.VMEM((seq, HEAD_DIM), F32)] * 3
        + [pltpu.VMEM((n_pat, seq, HEAD_DIM), F32)] * 3,
        compiler_params=_cparams(("parallel", "parallel")),
    )(slopes, *([z4] * (3 * n_pat)), q_gain, k_gain)


def _pool_kernel(u_ref, w_ref, sc_ref, o_ref):
    seq = u_ref.shape[2]
    t = lax.broadcasted_iota(jnp.int32, (seq, 1), 0)
    for g, w in enumerate(POOL_WINDOWS):
        u = u_ref[g, 0].astype(F32)
        acc = u
        span = 1
        while span < w:
            acc = acc + jnp.where(t >= span, pltpu.roll(acc, span, axis=0), 0.0)
            span *= 2
        d = acc / jnp.minimum(t + 1, w).astype(F32) - u
        y = _dot(d.astype(BF16), w_ref[g]) * sc_ref[:, g * LANES:(g + 1) * LANES]
        o_ref[:, g * LANES:(g + 1) * LANES] = y.astype(BF16)


def _pool(z4, w_pool, scale):
    _, bsz, seq, _ = z4.shape
    ng = len(POOL_WINDOWS)
    assert all(w & (w - 1) == 0 for w in POOL_WINDOWS) and POOL_SLAB0 % ng == 0
    return pl.pallas_call(
        _pool_kernel,
        grid=(bsz,),
        in_specs=[
            pl.BlockSpec((ng, 1, seq, LANES), lambda b: (POOL_SLAB0 // ng, b, 0, 0)),
            pl.BlockSpec((ng, LANES, LANES), lambda b: (0, 0, 0)),
            pl.BlockSpec((1, ng * LANES), lambda b: (0, 0)),
        ],
        out_specs=pl.BlockSpec((seq, ng * LANES), lambda b: (b, 0)),
        out_shape=jax.ShapeDtypeStruct((bsz * seq, ng * LANES), BF16),
        compiler_params=_cparams(("parallel",)),
    )(z4, w_pool, scale)


MERGE_TN = 512


def _merge_out_kernel(x_ref, ya_ref, yb_ref, yc_ref, g_ref, wa_ref, wb_ref, wc_ref, wo_ref, o_ref, m_ref):
    d = o_ref.shape[1]
    n_gate = d // LANES
    ya, yb, yc = ya_ref[...], yb_ref[...], yc_ref[...]
    for j in range(d // MERGE_TN):
        cols = slice(j * MERGE_TN, (j + 1) * MERGE_TN)
        a = _dot(ya, wa_ref[:, cols])
        b = _dot(yb, wb_ref[:, cols])
        c = _dot(yc, wc_ref[:, cols])
        for s in range(MERGE_TN // LANES):
            sl = slice(s * LANES, (s + 1) * LANES)
            slab = j * (MERGE_TN // LANES) + s
            m_ref[:, slab * LANES:(slab + 1) * LANES] = (
                g_ref[slab].astype(F32) * a[:, sl] + g_ref[n_gate + slab].astype(F32) * b[:, sl]
                + g_ref[2 * n_gate + slab].astype(F32) * c[:, sl]).astype(BF16)
    o_ref[...] = x_ref[...] + _dot(m_ref[...], wo_ref[...])


def _merge_out(x2, ya, yb, yc, wa, wb, wc, w_out, z, layer, tm=512):
    n, d = x2.shape
    act = lambda a: pl.BlockSpec((tm, a.shape[1]), lambda i: (i, 0))
    wgt = lambda w: pl.BlockSpec((None,) + w.shape[1:], lambda i: (layer, 0, 0), pipeline_mode=pl.Buffered(1))
    return pl.pallas_call(
        _merge_out_kernel,
        grid=(n // tm,),
        in_specs=[act(x2), act(ya), act(yb), act(yc),
                  pl.BlockSpec((MERGE_SLABS, tm, LANES), lambda i: (0, i, 0)),
                  wgt(wa), wgt(wb), wgt(wc), wgt(w_out)],
        out_specs=pl.BlockSpec((tm, d), lambda i: (i, 0)),
        out_shape=jax.ShapeDtypeStruct((n, d), F32),
        scratch_shapes=[pltpu.VMEM((tm, d), BF16)],
        compiler_params=_cparams(("parallel",)),
    )(x2, ya, yb, yc, z, wa, wb, wc, w_out)


FFN_TF = 512


def _ffn_kernel(x_ref, g_ref, wg_ref, wu_ref, wd_ref, o_ref, h_ref):
    f = pl.program_id(1)

    @pl.when(f == 0)
    def _():
        x = x_ref[...]
        h_ref[...] = _rms(x, g_ref[...]).astype(BF16)
        o_ref[...] = x

    h = h_ref[...]
    a = (jax.nn.silu(_dot(h, wg_ref[...])) * _dot(h, wu_ref[...])).astype(BF16)
    o_ref[...] += _dot(a, wd_ref[...])


def _ffn(x2, g, w_gate, w_up, w_down, layer, tm=512):
    n, d = x2.shape
    d_ff = w_gate.shape[2]
    return pl.pallas_call(
        _ffn_kernel,
        grid=(n // tm, d_ff // FFN_TF),
        in_specs=[
            pl.BlockSpec((tm, d), lambda i, f: (i, 0)),
            pl.BlockSpec((1, d), lambda i, f: (0, 0)),
            pl.BlockSpec((None, d, FFN_TF), lambda i, f: (layer, 0, f)),
            pl.BlockSpec((None, d, FFN_TF), lambda i, f: (layer, 0, f)),
            pl.BlockSpec((None, FFN_TF, d), lambda i, f: (layer, f, 0)),
        ],
        out_specs=pl.BlockSpec((tm, d), lambda i, f: (i, 0)),
        out_shape=jax.ShapeDtypeStruct((n, d), F32),
        scratch_shapes=[pltpu.VMEM((tm, d), BF16)],
        compiler_params=_cparams(("parallel", "arbitrary")),
    )(x2, g, w_gate, w_up, w_down)


def _router_kernel(x_ref, g_ref, w_ref, b_ref, idx_ref, wt_ref):
    h = _rms(x_ref[...], g_ref[...])
    h_hi = h.astype(BF16)
    h_lo = (h - h_hi.astype(F32)).astype(BF16)
    logits = _dot(h_hi, w_ref[0]) + (_dot(h_hi, w_ref[1]) + _dot(h_lo, w_ref[0])) + b_ref[...]
    lane = lax.broadcasted_iota(jnp.int32, logits.shape, 1)
    l1 = jnp.where(lane < N_EXPERTS, logits, -jnp.inf)
    m1 = jnp.max(l1, axis=-1, keepdims=True)
    i1 = jnp.min(jnp.where(l1 == m1, lane, LANES), axis=-1, keepdims=True)
    l2 = jnp.where(lane == i1, -jnp.inf, l1)
    m2 = jnp.max(l2, axis=-1, keepdims=True)
    i2 = jnp.min(jnp.where(l2 == m2, lane, LANES), axis=-1, keepdims=True)
    e = jnp.exp(m2 - m1)
    w1 = 1.0 / (1.0 + e)
    idx_ref[...] = jnp.where(lane == 0, i1, jnp.where(lane == 1, i2, 0))
    wt_ref[...] = jnp.where(lane == 0, w1, jnp.where(lane == 1, e * w1, 0.0))


def _router(x2, g, w_r, b_r, tm=512):
    n, d = x2.shape
    return pl.pallas_call(
        _router_kernel,
        grid=(n // tm,),
        in_specs=[
            pl.BlockSpec((tm, d), lambda i: (i, 0)),
            pl.BlockSpec((1, d), lambda i: (0, 0)),
            pl.BlockSpec((2, d, LANES), lambda i: (0, 0, 0)),
            pl.BlockSpec((1, LANES), lambda i: (0, 0)),
        ],
        out_specs=[pl.BlockSpec((tm, LANES), lambda i: (i, 0))] * 2,
        out_shape=[jax.ShapeDtypeStruct((n, LANES), jnp.int32), jax.ShapeDtypeStruct((n, LANES), F32)],
        compiler_params=_cparams(("parallel",)),
    )(x2, g, w_r, b_r)


MOE_TM = 512
MOE_TF = 512


def _moe_kernel(tile_e_ref, n_used_ref, row_tok_ref, x_hbm, g_ref, wg_ref, wu_ref, wd_ref, o_ref,
                xbuf, h_ref, sem):
    t = pl.program_id(0)
    f = pl.program_id(1)
    tm = MOE_TM
    n_used = n_used_ref[0]
    used = t < n_used
    slot = lax.rem(t, 2)

    def row_copy(tile, r, s):
        tok = row_tok_ref[tile * tm + r]
        return pltpu.make_async_copy(x_hbm.at[pl.ds(tok, 1), :], xbuf.at[s, pl.ds(r, 1), :], sem.at[s])

    def start_tile(tile, s):
        lax.fori_loop(0, tm, lambda r, c: (row_copy(tile, r, s).start(), c)[1], 0, unroll=8)

    def wait_tile(tile, s):
        lax.fori_loop(0, tm, lambda r, c: (row_copy(tile, r, s).wait(), c)[1], 0, unroll=8)

    @pl.when(f == 0)
    def _():
        o_ref[...] = jnp.zeros_like(o_ref)

    @pl.when((f == 0) & (t == 0))
    def _():
        start_tile(0, 0)

    @pl.when((f == 0) & used)
    def _():
        wait_tile(t, slot)

        @pl.when(t + 1 < n_used)
        def _():
            start_tile(t + 1, 1 - slot)

        h_ref[...] = _rms(xbuf[slot], g_ref[...]).astype(BF16)

    @pl.when(used)
    def _():
        h = h_ref[...]
        a = (jax.nn.silu(_dot(h, wg_ref[...])) * _dot(h, wu_ref[...])).astype(BF16)
        o_ref[...] += _dot(a, wd_ref[...])


def _moe_experts(x2, g, w_gate, w_up, w_down, layer, tile_e, n_used, row_tok):
    n, d = x2.shape
    d_ff = w_gate.shape[2]
    n_f = d_ff // MOE_TF
    n_tiles = tile_e.shape[0]
    e0 = layer * N_EXPERTS

    def f_eff(t, f, nu):
        return jnp.where(t < nu[0], f, n_f - 1)

    grid_spec = pltpu.PrefetchScalarGridSpec(
        num_scalar_prefetch=3,
        grid=(n_tiles, n_f),
        in_specs=[
            pl.BlockSpec(memory_space=pl.ANY),
            pl.BlockSpec((1, d), lambda t, f, te, nu, rt: (0, 0)),
            pl.BlockSpec((None, d, MOE_TF), lambda t, f, te, nu, rt: (e0 + te[t], 0, f_eff(t, f, nu))),
            pl.BlockSpec((None, d, MOE_TF), lambda t, f, te, nu, rt: (e0 + te[t], 0, f_eff(t, f, nu))),
            pl.BlockSpec((None, MOE_TF, d), lambda t, f, te, nu, rt: (e0 + te[t], f_eff(t, f, nu), 0)),
        ],
        out_specs=pl.BlockSpec((MOE_TM, d), lambda t, f, te, nu, rt: (t, 0)),
        scratch_shapes=[
            pltpu.VMEM((2, MOE_TM, d), F32),
            pltpu.VMEM((MOE_TM, d), BF16),
            pltpu.SemaphoreType.DMA((2,)),
        ],
    )
    return pl.pallas_call(
        _moe_kernel,
        grid_spec=grid_spec,
        out_shape=jax.ShapeDtypeStruct((n_tiles * MOE_TM, d), F32),
        compiler_params=_cparams(("arbitrary", "arbitrary")),
    )(tile_e, n_used, row_tok, x2, g, w_gate, w_up, w_down)


COMBINE_TM = 256


def _combine_kernel(pos_ref, x_ref, wt_ref, y_hbm, o_ref, buf, sem):
    i = pl.program_id(0)
    tm = COMBINE_TM
    slot = lax.rem(i, 2)

    def row_copy(tile, r, k, s):
        src = pos_ref[(tile * tm + r) * TOP_K + k]
        return pltpu.make_async_copy(y_hbm.at[pl.ds(src, 1), :], buf.at[s, k, pl.ds(r, 1), :], sem.at[s])

    def start_tile(tile, s):
        def body(r, c):
            for k in range(TOP_K):
                row_copy(tile, r, k, s).start()
            return c
        lax.fori_loop(0, tm, body, 0, unroll=8)

    def wait_tile(tile, s):
        def body(r, c):
            for k in range(TOP_K):
                row_copy(tile, r, k, s).wait()
            return c
        lax.fori_loop(0, tm, body, 0, unroll=8)

    @pl.when(i == 0)
    def _():
        start_tile(0, 0)

    wait_tile(i, slot)

    @pl.when(i + 1 < pl.num_programs(0))
    def _():
        start_tile(i + 1, 1 - slot)

    wt = wt_ref[...]
    acc = x_ref[...]
    for k in range(TOP_K):
        acc = acc + wt[:, k:k + 1] * buf[slot, k]
    o_ref[...] = acc


def _combine(x2, wts, ys, pos):
    n, d = x2.shape
    grid_spec = pltpu.PrefetchScalarGridSpec(
        num_scalar_prefetch=1,
        grid=(n // COMBINE_TM,),
        in_specs=[
            pl.BlockSpec((COMBINE_TM, d), lambda i, p: (i, 0)),
            pl.BlockSpec((COMBINE_TM, LANES), lambda i, p: (i, 0)),
            pl.BlockSpec(memory_space=pl.ANY),
        ],
        out_specs=pl.BlockSpec((COMBINE_TM, d), lambda i, p: (i, 0)),
        scratch_shapes=[pltpu.VMEM((2, TOP_K, COMBINE_TM, d), F32), pltpu.SemaphoreType.DMA((2,))],
    )
    return pl.pallas_call(
        _combine_kernel,
        grid_spec=grid_spec,
        out_shape=jax.ShapeDtypeStruct((n, d), F32),
        compiler_params=_cparams(("arbitrary",)),
    )(pos, x2, wts, ys)


def _route_plan(idx, n):
    flat_e = idx[:, :TOP_K].reshape(-1)
    onehot = (flat_e[:, None] == jnp.arange(N_EXPERTS, dtype=jnp.int32)[None, :]).astype(jnp.int32)
    before = jnp.cumsum(onehot, axis=0) - onehot
    rank = jnp.sum(before * onehot, axis=1)
    counts = jnp.sum(onehot, axis=0)
    padded = ((counts + MOE_TM - 1) // MOE_TM) * MOE_TM
    ends = jnp.cumsum(padded)
    pos = (ends - padded)[flat_e] + rank
    n_rows = TOP_K * n + N_EXPERTS * MOE_TM
    row_tok = jnp.zeros((n_rows,), jnp.int32).at[pos].set(jnp.arange(TOP_K * n, dtype=jnp.int32) // TOP_K)
    tile_start = jnp.arange(n_rows // MOE_TM, dtype=jnp.int32) * MOE_TM
    tile_e = jnp.sum((tile_start[:, None] >= ends[None, :]).astype(jnp.int32), axis=1)
    tile_e = jnp.minimum(tile_e, N_EXPERTS - 1)
    n_used = (ends[-1] // MOE_TM).astype(jnp.int32).reshape(1)
    return tile_e, n_used, row_tok, pos.astype(jnp.int32)


def _moe(x2, g, w_r, b_r, w_gate, w_up, w_down, layer):
    n = x2.shape[0]
    idx, wts = _router(x2, g, w_r, b_r)
    tile_e, n_used, row_tok, pos = _route_plan(idx, n)
    ys = _moe_experts(x2, g, w_gate, w_up, w_down, layer, tile_e, n_used, row_tok)
    return _combine(x2, wts, ys, pos)


def _prep_in_proj(w_in):
    d_model = w_in.shape[1]
    q_w = NSA_HEADS * HEAD_DIM
    kv_w = NSA_BRANCHES * 2 * NSA_KV_GROUPS * HEAD_DIM
    gate_w = NSA_BRANCHES * NSA_HEADS
    dil_w = 3 * DIL_HEADS * HEAD_DIM
    pool_w = len(POOL_WINDOWS) * LANES
    o_q, o_kv, o_g = 0, q_w, q_w + kv_w
    o_dil = o_g + gate_w
    o_pool = o_dil + dil_w
    o_merge = o_pool + pool_w
    assert w_in.shape[2] == o_merge + 3 * d_model
    w_bf = w_in.astype(BF16)
    w_main = jnp.concatenate(
        [w_bf[:, :, o_merge:], w_bf[:, :, o_q:o_g], w_bf[:, :, o_dil:o_merge]], axis=2)
    assert w_main.shape[2] == N_SLABS * LANES
    groups = []
    for g in range(NSA_KV_GROUPS):
        parts = []
        for br in range(NSA_BRANCHES):
            c0 = o_g + br * NSA_HEADS + g * NSA_HPG
            parts.append(w_bf[:, :, c0:c0 + NSA_HPG])
        parts.append(jnp.zeros(w_bf.shape[:2] + (LANES - NSA_BRANCHES * NSA_HPG,), BF16))
        groups.append(jnp.concatenate(parts, axis=2))
    w_gate = jnp.stack(groups, axis=1)
    return w_main, w_gate


def _sel_tables(seq):
    n_cmp = seq // CMP_STRIDE - 1
    n_sel = seq // SEL_BLOCK
    i = np.arange(n_cmp)[:, None] * CMP_STRIDE
    j = np.arange(n_sel)[None, :] * SEL_BLOCK
    ov = np.clip(np.minimum(i + CMP_BLOCK, j + SEL_BLOCK) - np.maximum(i, j), 0, None) / CMP_BLOCK
    ovt = np.zeros((n_sel, n_cmp + 1), np.float32)
    ovt[:, :n_cmp] = ov.T
    assert NSA_SEL_LANE0 + n_sel <= LANES
    ktab = _key_table(np.arange(seq), SEL_BLOCK)
    ctab = _key_table(np.arange(n_cmp + 1) * CMP_STRIDE + CMP_BLOCK - 1)
    return jnp.asarray(ovt), jnp.asarray(ktab, dtype=BF16), jnp.asarray(ctab, dtype=BF16)


def kernel(x, mix_norm, w_in, nsa_q_norm, nsa_k_norm, cmp_w1, cmp_b1, cmp_w2, dil_q_norm, dil_k_norm, pool_w, pool_scale, w_branch_a, w_branch_b, w_branch_c, w_out, ffn_norm, ffn_w_gate, ffn_w_up, ffn_w_down, moe_w_router, moe_b_router, moe_w_gate, moe_w_up, moe_w_down):
    bsz, seq, d_model = x.shape
    depth = w_in.shape[0]
    n = bsz * seq
    nsa_slopes = _alibi_slopes(NSA_HEADS)
    dil_slopes = _alibi_slopes(DIL_HEADS)
    ovt, ktab, ctab = _sel_tables(seq)
    x2 = x.reshape(n, d_model)
    w_main, w_gate = _prep_in_proj(w_in)
    wb_a, wb_b, wb_c, wb_out = (w.astype(BF16) for w in (w_branch_a, w_branch_b, w_branch_c, w_out))
    wf_g, wf_u, wf_d = (w.astype(BF16) for w in (ffn_w_gate, ffn_w_up, ffn_w_down))
    wm_g, wm_u, wm_d = (w.astype(BF16).reshape((-1,) + w.shape[2:]) for w in (moe_w_gate, moe_w_up, moe_w_down))
    cw1, cw2, pw = cmp_w1.astype(BF16), cmp_w2.astype(BF16), pool_w.astype(BF16)
    for l in range(depth):
        z, gates = _in_proj(x2, mix_norm[l][None, :], w_main, w_gate, l)
        z4 = z.reshape(N_SLABS, bsz, seq, LANES)
        y_a = _nsa(z4, gates, nsa_q_norm[l][None, :], nsa_k_norm[l], cw1[l], cmp_b1[l], cw2[l],
                   nsa_slopes, ovt, ktab, ctab)
        y_b = _dil(z4, dil_q_norm[l][None, :], dil_k_norm[l][None, :], dil_slopes)
        y_c = _pool(z4, pw[l], pool_scale[l][None, :])
        x2 = _merge_out(x2, y_a, y_b, y_c, wb_a, wb_b, wb_c, wb_out, z, l)
        i = l // 2
        if l % 2 == 0:
            x2 = _ffn(x2, ffn_norm[l][None, :], wf_g, wf_u, wf_d, i)
        else:
            w_r = jnp.zeros((d_model, LANES), F32).at[:, :N_EXPERTS].set(moe_w_router[i])
            w_hi = w_r.astype(BF16)
            w_r = jnp.stack([w_hi, (w_r - w_hi.astype(F32)).astype(BF16)])
            b_r = jnp.zeros((1, LANES), F32).at[0, :N_EXPERTS].set(moe_b_router[i])
            x2 = _moe(x2, ffn_norm[l][None, :], w_r, b_r, wm_g, wm_u, wm_d, i)
    return x2.reshape(bsz, seq, d_model)
```
